```python
import math
import jax, jax.numpy as jnp
from jax import lax
import numpy as np

D_MODEL = 1024
BATCH = 8
SEQ = 2048
DEPTH = 2
DEC_BATCH = 2
DEC_SEQ = 8192
PAST_LEN = 128

N_MIXERS = 2
N_ATTN_LAYERS = (DEPTH + 1) // 2
N_CONV_LAYERS = DEPTH // 2
HEAD_DIM = 64
N_HEADS = D_MODEL // HEAD_DIM
N_KV_HEADS = 4
GROUP = N_HEADS // N_KV_HEADS
Q_DIM = N_HEADS * HEAD_DIM
KV_DIM = N_KV_HEADS * HEAD_DIM
QKV_DIM = Q_DIM + 2 * KV_DIM
Q_BLOCK = 128
GRID_W = 64
ROPE_AXIS_DIM = HEAD_DIM // 2
ROPE_THETA = 10000.0
CONV_WIDTH = 31
CONV_PAD = CONV_WIDTH // 2
FFN_DIM = ((8 * D_MODEL + 3 * 256 - 1) // (3 * 256)) * 256
RMS_EPS = 1e-6
LN_EPS = 1e-5

kernel_name = "hybrid_axial_gqa_conformer_encoder"


def _rms_norm(x, g):
    xf = x.astype(jnp.float32)
    y = xf * lax.rsqrt(jnp.mean(xf * xf, axis=-1, keepdims=True) + RMS_EPS)
    return (y * g.astype(jnp.float32)).astype(x.dtype)


def _layer_norm(x, g, b):
    xf = x.astype(jnp.float32)
    mu = jnp.mean(xf, axis=-1, keepdims=True)
    xc = xf - mu
    var = jnp.mean(xc * xc, axis=-1, keepdims=True)
    y = xc * lax.rsqrt(var + LN_EPS)
    return (y * g.astype(jnp.float32) + b.astype(jnp.float32)).astype(x.dtype)


def _axial_rope_tables(seq_len):
    rows = seq_len // GRID_W
    row_ids = jnp.repeat(jnp.arange(rows, dtype=jnp.float32), GRID_W)
    col_ids = jnp.tile(jnp.arange(GRID_W, dtype=jnp.float32), rows)
    inv_freq = ROPE_THETA ** (-jnp.arange(0, ROPE_AXIS_DIM, 2, dtype=jnp.float32) / ROPE_AXIS_DIM)
    ang_r = row_ids[:, None] * inv_freq[None, :]
    ang_c = col_ids[:, None] * inv_freq[None, :]
    return jnp.cos(ang_r), jnp.sin(ang_r), jnp.cos(ang_c), jnp.sin(ang_c)


def _rotate_half(x, cos, sin):
    f = x.shape[-1] // 2
    x1, x2 = x[..., :f], x[..., f:]
    c = cos[:, None, :]
    s = sin[:, None, :]
    return jnp.concatenate([x1 * c - x2 * s, x2 * c + x1 * s], axis=-1)


def _apply_axial_rope(x, tables):
    cos_r, sin_r, cos_c, sin_c = tables
    xf = x.astype(jnp.float32)
    xr = _rotate_half(xf[..., :ROPE_AXIS_DIM], cos_r, sin_r)
    xc = _rotate_half(xf[..., ROPE_AXIS_DIM:], cos_c, sin_c)
    return jnp.concatenate([xr, xc], axis=-1).astype(x.dtype)


def _attention(h, w_qkv, q_norm_g, k_norm_g, w_o):
    B, S, _ = h.shape
    qkv = h @ w_qkv
    q = qkv[..., :Q_DIM].reshape(B, S, N_HEADS, HEAD_DIM)
    k = qkv[..., Q_DIM:Q_DIM + KV_DIM].reshape(B, S, N_KV_HEADS, HEAD_DIM)
    v = qkv[..., Q_DIM + KV_DIM:].reshape(B, S, N_KV_HEADS, HEAD_DIM)
    tables = _axial_rope_tables(S)
    scale = 1.0 / math.sqrt(HEAD_DIM)
    q = _apply_axial_rope(_rms_norm(q, q_norm_g), tables) * jnp.asarray(scale, dtype=h.dtype)
    k = _apply_axial_rope(_rms_norm(k, k_norm_g), tables)
    n_blk = S // Q_BLOCK
    qb = q.reshape(B, n_blk, Q_BLOCK, N_KV_HEADS, GROUP, HEAD_DIM).transpose(1, 0, 2, 3, 4, 5)

    def one_block(q_blk):
        s = jnp.einsum('bqkgd,bskd->bkgqs', q_blk, k, preferred_element_type=jnp.float32)
        p = jax.nn.softmax(s, axis=-1).astype(v.dtype)
        return jnp.einsum('bkgqs,bskd->bqkgd', p, v)

    o = lax.map(one_block, qb)
    o = o.transpose(1, 0, 2, 3, 4, 5).reshape(B, S, Q_DIM)
    return o @ w_o


def _conformer_conv(h, w_in, b_in, dw_w, dw_b, ln_g, ln_b, w_out, b_out):
    C = h.shape[-1]
    u = h @ w_in + b_in
    a, gate = u[..., :C], u[..., C:]
    u = a * jax.nn.sigmoid(gate)
    u = lax.conv_general_dilated(
        u, dw_w[:, None, :], window_strides=(1,), padding=[(CONV_PAD, CONV_PAD)],
        dimension_numbers=('NWC', 'WIO', 'NWC'), feature_group_count=C) + dw_b
    u = jax.nn.silu(_layer_norm(u, ln_g, ln_b))
    return u @ w_out + b_out


def _swiglu(h, w_gate, w_up, w_down):
    return (jax.nn.silu(h @ w_gate) * (h @ w_up)) @ w_down


def _trunk(x, attn_norm_g, w_qkv, q_norm_g, k_norm_g, w_o,
           conv_norm_g, conv_w_in, conv_b_in, dw_w, dw_b, conv_ln_g, conv_ln_b, conv_w_out, conv_b_out,
           ffn_norm_g, w_gate, w_up, w_down, final_norm_g):
    for i in range(DEPTH):
        j = i // N_MIXERS
        if i % N_MIXERS == 0:
            x = x + _attention(_rms_norm(x, attn_norm_g[j]), w_qkv[j], q_norm_g[j], k_norm_g[j], w_o[j])
        else:
            x = x + _conformer_conv(_rms_norm(x, conv_norm_g[j]), conv_w_in[j], conv_b_in[j], dw_w[j], dw_b[j],
                                    conv_ln_g[j], conv_ln_b[j], conv_w_out[j], conv_b_out[j])
        x = x + _swiglu(_rms_norm(x, ffn_norm_g[i]), w_gate[i], w_up[i], w_down[i])
    return _rms_norm(x, final_norm_g)


def setup_inputs(seed: int = 0) -> dict:
    key = jax.random.key(seed)
    ks = jax.random.split(key, 24)
    f32 = jnp.float32

    def w(k, shape, fan_in):
        return jax.random.normal(k, shape, f32) * (fan_in ** -0.5)

    def gain(k, shape):
        return 1.0 + 0.02 * jax.random.normal(k, shape, f32)

    def bias(k, shape):
        return 0.02 * jax.random.normal(k, shape, f32)

    D, F, NA, NC = D_MODEL, FFN_DIM, N_ATTN_LAYERS, N_CONV_LAYERS
    return {
        "x_prompt": jax.random.normal(ks[0], (BATCH, SEQ, D), f32),
        "x_sample": jax.random.normal(ks[1], (DEC_BATCH, DEC_SEQ, D), f32),
        "attn_norm_g": gain(ks[2], (NA, D)),
        "w_qkv": w(ks[3], (NA, D, QKV_DIM), D),
        "q_norm_g": gain(ks[4], (NA, HEAD_DIM)),
        "k_norm_g": gain(ks[5], (NA, HEAD_DIM)),
        "w_o": w(ks[6], (NA, Q_DIM, D), Q_DIM),
        "conv_norm_g": gain(ks[7], (NC, D)),
        "conv_w_in": w(ks[8], (NC, D, 2 * D), D),
        "conv_b_in": bias(ks[9], (NC, 2 * D)),
        "dw_w": w(ks[10], (NC, CONV_WIDTH, D), CONV_WIDTH),
        "dw_b": bias(ks[11], (NC, D)),
        "conv_ln_g": gain(ks[12], (NC, D)),
        "conv_ln_b": bias(ks[13], (NC, D)),
        "conv_w_out": w(ks[14], (NC, D, D), D),
        "conv_b_out": bias(ks[15], (NC, D)),
        "ffn_norm_g": gain(ks[16], (DEPTH, D)),
        "w_gate": w(ks[17], (DEPTH, D, F), D),
        "w_up": w(ks[18], (DEPTH, D, F), D),
        "w_down": w(ks[19], (DEPTH, F, D), F),
        "final_norm_g": gain(ks[20], (D,)),
    }


def reference(x_prompt, x_sample, attn_norm_g, w_qkv, q_norm_g, k_norm_g, w_o,
              conv_norm_g, conv_w_in, conv_b_in, dw_w, dw_b, conv_ln_g, conv_ln_b, conv_w_out, conv_b_out,
              ffn_norm_g, w_gate, w_up, w_down, final_norm_g):
    y_prompt = _trunk(x_prompt, attn_norm_g, w_qkv, q_norm_g, k_norm_g, w_o,
                      conv_norm_g, conv_w_in, conv_b_in, dw_w, dw_b, conv_ln_g, conv_ln_b, conv_w_out, conv_b_out,
                      ffn_norm_g, w_gate, w_up, w_down, final_norm_g)
    y_sample = _trunk(x_sample, attn_norm_g, w_qkv, q_norm_g, k_norm_g, w_o,
                      conv_norm_g, conv_w_in, conv_b_in, dw_w, dw_b, conv_ln_g, conv_ln_b, conv_w_out, conv_b_out,
                      ffn_norm_g, w_gate, w_up, w_down, final_norm_g)
    return (y_prompt, y_sample)
```

```python
import functools
import math

import jax
import jax.numpy as jnp
from jax import lax
from jax.experimental import pallas as pl
from jax.experimental.pallas import tpu as pltpu

HEAD_DIM = 64
N_HEADS = 16
N_KV_HEADS = 4
GROUP = N_HEADS // N_KV_HEADS
GRID_W = 64
ROPE_AXIS_DIM = HEAD_DIM // 2
ROPE_HALF = ROPE_AXIS_DIM // 2
ROPE_THETA = 10000.0
CONV_WIDTH = 31
CONV_PAD = CONV_WIDTH // 2
RMS_EPS = 1e-6
LN_EPS = 1e-5

LANES = 128
MXU_N = 256
HALO = 16
VMEM_LIMIT = 56 * 1024 * 1024

F32 = jnp.float32
BF16 = jnp.bfloat16


def _const_spec(shape):
    zeros = (0,) * len(shape)
    return pl.BlockSpec(shape, lambda *_: zeros, pipeline_mode=pl.Buffered(1))


def _rms(x, g):
    ms = jnp.mean(x * x, axis=-1, keepdims=True)
    return x * lax.rsqrt(ms + RMS_EPS) * g


def _qkv_kernel(x_ref, g_ref, w_ref, gqk_ref, seg_ref, cos_ref, sa_ref, sb_ref,
                qT_ref, k_ref, vT_ref):
    q_dim = N_HEADS * HEAD_DIM
    qk_dim = q_dim + N_KV_HEADS * HEAD_DIM
    h = _rms(x_ref[0], g_ref[...]).astype(BF16)
    qkv = jnp.dot(h, w_ref[...], preferred_element_type=F32)
    seg = seg_ref[...]
    cos, sa, sb = cos_ref[...], sa_ref[...], sb_ref[...]
    for c in range(qk_dim // MXU_N):
        z = qkv[:, c * MXU_N:(c + 1) * MXU_N]
        y = z * z
        y_hi = y.astype(BF16)
        y_lo = (y - y_hi.astype(F32)).astype(BF16)
        ms = (jnp.dot(y_hi, seg, preferred_element_type=F32)
              + jnp.dot(y_lo, seg, preferred_element_type=F32))
        zn = z * lax.rsqrt(ms + RMS_EPS) * gqk_ref[:, c * MXU_N:(c + 1) * MXU_N]
        for half in range(MXU_N // LANES):
            zz = zn[:, half * LANES:(half + 1) * LANES]
            r = (zz * cos
                 + pltpu.roll(zz, ROPE_HALF, 1) * sa
                 + pltpu.roll(zz, LANES - ROPE_HALF, 1) * sb)
            col = c * MXU_N + half * LANES
            if col < q_dim:
                qT_ref[0, col:col + LANES, :] = r.T.astype(BF16)
            else:
                kv = (col - q_dim) // HEAD_DIM
                k_ref[0, kv] = r[:, :HEAD_DIM].astype(BF16)
                k_ref[0, kv + 1] = r[:, HEAD_DIM:].astype(BF16)
    vT_ref[0] = qkv[:, qk_dim:].T.astype(BF16)


def _qkv_call(x, g, w_qkv, gqk, seg, cos, sa, sb, tm):
    B, S, D = x.shape
    n_s = S // tm
    qkv_dim = w_qkv.shape[1]
    kv_dim = N_KV_HEADS * HEAD_DIM
    tab_spec = pl.BlockSpec((tm, LANES), lambda b, i: (i, 0))
    return pl.pallas_call(
        _qkv_kernel,
        grid=(B, n_s),
        in_specs=[
            pl.BlockSpec((1, tm, D), lambda b, i: (b, i, 0)),
            _const_spec((1, D)),
            _const_spec((D, qkv_dim)),
            _const_spec((1, gqk.shape[1])),
            _const_spec((MXU_N, MXU_N)),
            tab_spec, tab_spec, tab_spec,
        ],
        out_specs=[
            pl.BlockSpec((1, N_HEADS * HEAD_DIM, tm), lambda b, i: (b, 0, i)),
            pl.BlockSpec((1, N_KV_HEADS, tm, HEAD_DIM), lambda b, i: (b, 0, i, 0)),
            pl.BlockSpec((1, kv_dim, tm), lambda b, i: (b, 0, i)),
        ],
        out_shape=[
            jax.ShapeDtypeStruct((B, N_HEADS * HEAD_DIM, S), BF16),
            jax.ShapeDtypeStruct((B, N_KV_HEADS, S, HEAD_DIM), BF16),
            jax.ShapeDtypeStruct((B, kv_dim, S), BF16),
        ],
        compiler_params=pltpu.CompilerParams(
            dimension_semantics=("parallel", "parallel"),
            vmem_limit_bytes=VMEM_LIMIT),
        name="qkv_rope",
    )(x, g, w_qkv, gqk, seg, cos, sa, sb)


def _flash_kernel(qT_ref, k_ref, vT_ref, o_ref, m_ref, l_ref, acc_ref, *, tk):
    S = k_ref.shape[2]
    m_ref[...] = jnp.full(m_ref.shape, -jnp.inf, F32)
    l_ref[...] = jnp.zeros(l_ref.shape, F32)
    acc_ref[...] = jnp.zeros(acc_ref.shape, F32)

    def body(c, carry):
        start = pl.multiple_of(c * tk, tk)
        k = k_ref[0, 0, pl.ds(start, tk), :]
        vT = vT_ref[0, :, pl.ds(start, tk)]
        for h in range(GROUP):
            rows = slice(h * HEAD_DIM, (h + 1) * HEAD_DIM)
            sT = jnp.dot(k, qT_ref[0, rows, :], preferred_element_type=F32)
            m_old = m_ref[h]
            m_new = jnp.maximum(m_old, jnp.max(sT, axis=0, keepdims=True))
            alpha = jnp.exp(m_old - m_new)
            p = jnp.exp(sT - m_new)
            l_ref[h] = alpha * l_ref[h] + jnp.sum(p, axis=0, keepdims=True)
            acc_ref[rows, :] = alpha * acc_ref[rows, :] + jnp.dot(
                vT, p.astype(BF16), preferred_element_type=F32)
            m_ref[h] = m_new
        return carry

    lax.fori_loop(0, S // tk, body, 0)
    for h in range(GROUP):
        rows = slice(h * HEAD_DIM, (h + 1) * HEAD_DIM)
        acc_ref[rows, :] = acc_ref[rows, :] * (1.0 / l_ref[h])
    o_ref[0] = acc_ref[...].T.astype(BF16)


def _flash_call(qT, k, vT, tq, tk):
    B, _, S = qT.shape
    gw = GROUP * HEAD_DIM
    return pl.pallas_call(
        functools.partial(_flash_kernel, tk=tk),
        grid=(B, N_KV_HEADS, S // tq),
        in_specs=[
            pl.BlockSpec((1, gw, tq), lambda b, j, i: (b, j, i)),
            pl.BlockSpec((1, 1, S, HEAD_DIM), lambda b, j, i: (b, j, 0, 0)),
            pl.BlockSpec((1, HEAD_DIM, S), lambda b, j, i: (b, j, 0)),
        ],
        out_specs=pl.BlockSpec((1, tq, gw), lambda b, j, i: (b, i, j)),
        out_shape=jax.ShapeDtypeStruct((B, S, N_HEADS * HEAD_DIM), BF16),
        scratch_shapes=[
            pltpu.VMEM((GROUP, 1, tq), F32),
            pltpu.VMEM((GROUP, 1, tq), F32),
            pltpu.VMEM((gw, tq), F32),
        ],
        compiler_params=pltpu.CompilerParams(
            dimension_semantics=("parallel", "parallel", "parallel"),
            vmem_limit_bytes=VMEM_LIMIT),
        name="flash_attn_t",
    )(qT, k, vT)


def _ffn(x, g_ref, wg_ref, wu_ref, wd_ref, act_ref):
    h = _rms(x, g_ref[...]).astype(BF16)
    f = wg_ref.shape[1]
    for c in range(f // MXU_N):
        cols = slice(c * MXU_N, (c + 1) * MXU_N)
        gate = jnp.dot(h, wg_ref[:, cols], preferred_element_type=F32)
        up = jnp.dot(h, wu_ref[:, cols], preferred_element_type=F32)
        act_ref[:, cols] = (gate * jax.nn.sigmoid(gate) * up).astype(BF16)
    return x + jnp.dot(act_ref[...], wd_ref[...], preferred_element_type=F32)


def _attn_out_ffn_kernel(x_ref, o_ref, wo_ref, g_ref, wg_ref, wu_ref, wd_ref,
                         y_ref, act_ref):
    x1 = x_ref[...] + jnp.dot(o_ref[...], wo_ref[...], preferred_element_type=F32)
    y_ref[...] = _ffn(x1, g_ref, wg_ref, wu_ref, wd_ref, act_ref)


def _attn_out_ffn_call(x, o, wo, g, wg, wu, wd, tm):
    T, D = x.shape
    f = wg.shape[1]
    row_spec = pl.BlockSpec((tm, D), lambda i: (i, 0))
    return pl.pallas_call(
        _attn_out_ffn_kernel,
        grid=(T // tm,),
        in_specs=[row_spec, row_spec, _const_spec((D, D)), _const_spec((1, D)),
                  _const_spec((D, f)), _const_spec((D, f)), _const_spec((f, D))],
        out_specs=row_spec,
        out_shape=jax.ShapeDtypeStruct((T, D), F32),
        scratch_shapes=[pltpu.VMEM((tm, f), BF16)],
        compiler_params=pltpu.CompilerParams(
            dimension_semantics=("parallel",),
            vmem_limit_bytes=VMEM_LIMIT),
        name="attn_out_ffn",
    )(x, o, wo, g, wg, wu, wd)


def _conv_ffn_kernel(xm_ref, xp_ref, xn_ref, cg_ref, win_ref, bin_ref, dww_ref, dwb_ref,
                     lng_ref, lnb_ref, wout_ref, bout_ref, fg_ref, wg_ref, wu_ref, wd_ref,
                     fin_ref, y_ref, u_ref, c_ref, act_ref, *, rc):
    ts, D = xm_ref.shape[1], xm_ref.shape[2]
    i = pl.program_id(1)
    n = pl.num_programs(1)
    xm = xm_ref[0]
    xe = jnp.concatenate([xp_ref[0], xm, xn_ref[0]], axis=0)
    h = _rms(xe, cg_ref[...]).astype(BF16)
    u = jnp.dot(h, win_ref[...], preferred_element_type=F32) + bin_ref[...]
    u = u[:, :D] * jax.nn.sigmoid(u[:, D:])
    row = lax.broadcasted_iota(jnp.int32, (ts + 2 * HALO, 1), 0)
    valid = jnp.logical_and(jnp.logical_or(row >= HALO, i > 0),
                            jnp.logical_or(row < ts + HALO, i < n - 1))
    u_ref[...] = jnp.where(valid, u, 0.0)
    off = HALO - CONV_PAD
    for r in range(ts // rc):
        for c in range(D // LANES):
            cols = slice(c * LANES, (c + 1) * LANES)
            acc = jnp.broadcast_to(dwb_ref[:, cols], (rc, LANES))
            for k in range(CONV_WIDTH):
                acc = acc + u_ref[pl.ds(r * rc + k + off, rc), cols] * dww_ref[k:k + 1, cols]
            c_ref[r * rc:(r + 1) * rc, cols] = acc
    v = c_ref[...]
    mu = jnp.mean(v, axis=-1, keepdims=True)
    vc = v - mu
    var = jnp.mean(vc * vc, axis=-1, keepdims=True)
    v = vc * lax.rsqrt(var + LN_EPS) * lng_ref[...] + lnb_ref[...]
    v = (v * jax.nn.sigmoid(v)).astype(BF16)
    x3 = xm + jnp.dot(v, wout_ref[...], preferred_element_type=F32) + bout_ref[...]
    x4 = _ffn(x3, fg_ref, wg_ref, wu_ref, wd_ref, act_ref)
    y_ref[0] = _rms(x4, fin_ref[...])


def _conv_ffn_call(x, cg, win, bin_, dww, dwb, lng, lnb, wout, bout, fg, wg, wu, wd, fin, ts, rc):
    B, S, D = x.shape
    f = wg.shape[1]
    hb = ts // HALO
    n_hb = S // HALO
    return pl.pallas_call(
        functools.partial(_conv_ffn_kernel, rc=rc),
        grid=(B, S // ts),
        in_specs=[
            pl.BlockSpec((1, ts, D), lambda b, i: (b, i, 0)),
            pl.BlockSpec((1, HALO, D), lambda b, i: (b, jnp.maximum(i * hb - 1, 0), 0)),
            pl.BlockSpec((1, HALO, D), lambda b, i: (b, jnp.minimum((i + 1) * hb, n_hb - 1), 0)),
            _const_spec((1, D)), _const_spec((D, 2 * D)), _const_spec((1, 2 * D)),
            _const_spec(dww.shape), _const_spec((1, D)), _const_spec((1, D)), _const_spec((1, D)),
            _const_spec((D, D)), _const_spec((1, D)), _const_spec((1, D)),
            _const_spec((D, f)), _const_spec((D, f)), _const_spec((f, D)), _const_spec((1, D)),
        ],
        out_specs=pl.BlockSpec((1, ts, D), lambda b, i: (b, i, 0)),
        out_shape=jax.ShapeDtypeStruct((B, S, D), F32),
        scratch_shapes=[
            pltpu.VMEM((ts + 2 * HALO, D), F32),
            pltpu.VMEM((ts, D), F32),
            pltpu.VMEM((ts, f), BF16),
        ],
        compiler_params=pltpu.CompilerParams(
            dimension_semantics=("parallel", "parallel"),
            vmem_limit_bytes=VMEM_LIMIT),
        name="conv_ffn_final",
    )(x, x, x, cg, win, bin_, dww, dwb, lng, lnb, wout, bout, fg, wg, wu, wd, fin)


def _rope_tables(seq_len):
    t = jnp.arange(seq_len, dtype=jnp.int32)
    row_ids = (t // GRID_W).astype(F32)
    col_ids = (t % GRID_W).astype(F32)
    inv_freq = ROPE_THETA ** (-jnp.arange(0, ROPE_AXIS_DIM, 2, dtype=F32) / ROPE_AXIS_DIM)
    ang_r = row_ids[:, None] * inv_freq[None, :]
    ang_c = col_ids[:, None] * inv_freq[None, :]
    zero = jnp.zeros_like(ang_r)
    cos = jnp.concatenate([jnp.cos(ang_r)] * 2 + [jnp.cos(ang_c)] * 2, axis=-1)
    sa = jnp.concatenate([zero, jnp.sin(ang_r), zero, jnp.sin(ang_c)], axis=-1)
    sb = jnp.concatenate([-jnp.sin(ang_r), zero, -jnp.sin(ang_c), zero], axis=-1)
    reps = LANES // HEAD_DIM
    return tuple(jnp.tile(a, (1, reps)) for a in (cos, sa, sb))


def _trunk(x, p, tables):
    B, S, D = x.shape
    cos, sa, sb = (t[:S] for t in tables)
    qT, k, vT = _qkv_call(x, p["attn_g"], p["w_qkv"], p["gqk"], p["seg"], cos, sa, sb, tm=512)
    o = _flash_call(qT, k, vT, tq=256, tk=256)
    x2 = _attn_out_ffn_call(x.reshape(B * S, D), o.reshape(B * S, D), p["w_o"], p["ffn_g0"],
                            p["wg0"], p["wu0"], p["wd0"], tm=512)
    return _conv_ffn_call(x2.reshape(B, S, D), p["conv_g"], p["w_in"], p["b_in"], p["dw_w"],
                          p["dw_b"], p["ln_g"], p["ln_b"], p["w_out"], p["b_out"], p["ffn_g1"],
                          p["wg1"], p["wu1"], p["wd1"], p["fin_g"], ts=512, rc=64)


def kernel(x_prompt, x_sample, attn_norm_g, w_qkv, q_norm_g, k_norm_g, w_o, conv_norm_g, conv_w_in, conv_b_in, dw_w, dw_b, conv_ln_g, conv_ln_b, conv_w_out, conv_b_out, ffn_norm_g, w_gate, w_up, w_down, final_norm_g):
    row = lambda v: v.reshape(1, -1).astype(F32)
    scale = 1.0 / math.sqrt(HEAD_DIM)
    gqk = jnp.concatenate([jnp.tile(q_norm_g[0] * scale, N_HEADS),
                           jnp.tile(k_norm_g[0], N_KV_HEADS)]).reshape(1, -1)
    lane = jnp.arange(MXU_N) // HEAD_DIM
    seg = jnp.where(lane[:, None] == lane[None, :], 1.0 / HEAD_DIM, 0.0).astype(BF16)
    p = dict(
        attn_g=row(attn_norm_g[0]), w_qkv=w_qkv[0].astype(BF16), gqk=gqk, seg=seg,
        w_o=w_o[0].astype(BF16), ffn_g0=row(ffn_norm_g[0]),
        wg0=w_gate[0].astype(BF16), wu0=w_up[0].astype(BF16), wd0=w_down[0].astype(BF16),
        conv_g=row(conv_norm_g[0]), w_in=conv_w_in[0].astype(BF16), b_in=row(conv_b_in[0]),
        dw_w=jnp.pad(dw_w[0], ((0, 1), (0, 0))), dw_b=row(dw_b[0]),
        ln_g=row(conv_ln_g[0]), ln_b=row(conv_ln_b[0]),
        w_out=conv_w_out[0].astype(BF16), b_out=row(conv_b_out[0]), ffn_g1=row(ffn_norm_g[1]),
        wg1=w_gate[1].astype(BF16), wu1=w_up[1].astype(BF16), wd1=w_down[1].astype(BF16),
        fin_g=row(final_norm_g),
    )
    tables = _rope_tables(max(x_prompt.shape[1], x_sample.shape[1]))
    return (_trunk(x_prompt, p, tables), _trunk(x_sample, p, tables))
```

```python
import functools
import math

import jax
import jax.numpy as jnp
from jax import lax
from jax.experimental import pallas as pl
from jax.experimental.pallas import tpu as pltpu

HEAD_DIM = 64
N_HEADS = 16
N_KV_HEADS = 4
GROUP = N_HEADS // N_KV_HEADS
GRID_W = 64
ROPE_AXIS_DIM = HEAD_DIM // 2
ROPE_HALF = ROPE_AXIS_DIM // 2
ROPE_THETA = 10000.0
CONV_WIDTH = 31
CONV_PAD = CONV_WIDTH // 2
RMS_EPS = 1e-6
LN_EPS = 1e-5

LANES = 128
MXU_N = 256
HALO = 16
VMEM_LIMIT = 56 * 1024 * 1024

F32 = jnp.float32
BF16 = jnp.bfloat16


def _const_spec(shape):
    zeros = (0,) * len(shape)
    return pl.BlockSpec(shape, lambda *_: zeros, pipeline_mode=pl.Buffered(1))


def _rms(x, g):
    ms = jnp.mean(x * x, axis=-1, keepdims=True)
    return x * lax.rsqrt(ms + RMS_EPS) * g


def _qkv_kernel(x_ref, g_ref, w_ref, gqk_ref, seg_ref, cos_ref, sa_ref, sb_ref,
                qT_ref, k_ref, vT_ref, *, tq):
    tm = x_ref.shape[1]
    q_dim = N_HEADS * HEAD_DIM
    qk_dim = q_dim + N_KV_HEADS * HEAD_DIM
    h = _rms(x_ref[0], g_ref[...]).astype(BF16)
    qkv = jnp.dot(h, w_ref[...], preferred_element_type=F32)
    seg = seg_ref[...]
    cos, sa, sb = cos_ref[...], sa_ref[...], sb_ref[...]
    for c in range(qk_dim // MXU_N):
        z = qkv[:, c * MXU_N:(c + 1) * MXU_N]
        y = z * z
        y_hi = y.astype(BF16)
        y_lo = (y - y_hi.astype(F32)).astype(BF16)
        ms = (jnp.dot(y_hi, seg, preferred_element_type=F32)
              + jnp.dot(y_lo, seg, preferred_element_type=F32))
        zn = z * lax.rsqrt(ms + RMS_EPS) * gqk_ref[:, c * MXU_N:(c + 1) * MXU_N]
        for half in range(MXU_N // LANES):
            zz = zn[:, half * LANES:(half + 1) * LANES]
            r = (zz * cos
                 + pltpu.roll(zz, ROPE_HALF, 1) * sa
                 + pltpu.roll(zz, LANES - ROPE_HALF, 1) * sb)
            col = c * MXU_N + half * LANES
            if col < q_dim:
                rT = r.T.astype(BF16)
                for hh in range(LANES // HEAD_DIM):
                    g = half * (LANES // HEAD_DIM) + hh
                    for tt in range(tm // tq):
                        dst = (tt * GROUP + g) * tq
                        qT_ref[0, c, :, dst:dst + tq] = rT[hh * HEAD_DIM:(hh + 1) * HEAD_DIM,
                                                           tt * tq:(tt + 1) * tq]
            else:
                kv = (col - q_dim) // HEAD_DIM
                k_ref[0, kv] = r[:, :HEAD_DIM].astype(BF16)
                k_ref[0, kv + 1] = r[:, HEAD_DIM:].astype(BF16)
    vT_ref[0] = qkv[:, qk_dim:].T.astype(BF16)


def _qkv_call(x, g, w_qkv, gqk, seg, cos, sa, sb, tm, tq):
    B, S, D = x.shape
    n_s = S // tm
    qkv_dim = w_qkv.shape[1]
    kv_dim = N_KV_HEADS * HEAD_DIM
    tab_spec = pl.BlockSpec((tm, LANES), lambda b, i: (i, 0))
    return pl.pallas_call(
        functools.partial(_qkv_kernel, tq=tq),
        grid=(B, n_s),
        in_specs=[
            pl.BlockSpec((1, tm, D), lambda b, i: (b, i, 0)),
            _const_spec((1, D)),
            _const_spec((D, qkv_dim)),
            _const_spec((1, gqk.shape[1])),
            _const_spec((MXU_N, MXU_N)),
            tab_spec, tab_spec, tab_spec,
        ],
        out_specs=[
            pl.BlockSpec((1, N_KV_HEADS, HEAD_DIM, GROUP * tm), lambda b, i: (b, 0, 0, i)),
            pl.BlockSpec((1, N_KV_HEADS, tm, HEAD_DIM), lambda b, i: (b, 0, i, 0)),
            pl.BlockSpec((1, kv_dim, tm), lambda b, i: (b, 0, i)),
        ],
        out_shape=[
            jax.ShapeDtypeStruct((B, N_KV_HEADS, HEAD_DIM, GROUP * S), BF16),
            jax.ShapeDtypeStruct((B, N_KV_HEADS, S, HEAD_DIM), BF16),
            jax.ShapeDtypeStruct((B, kv_dim, S), BF16),
        ],
        compiler_params=pltpu.CompilerParams(
            dimension_semantics=("parallel", "parallel"),
            vmem_limit_bytes=VMEM_LIMIT),
        name="qkv_rope",
    )(x, g, w_qkv, gqk, seg, cos, sa, sb)


NEG_BIG = -1e30


def _flash_kernel(qT_ref, k_ref, vT_ref, o_ref, s_ref, p_ref, mx_ref, m_ref, l_ref, acc_ref,
                  *, tk, tq):
    S = k_ref.shape[2]
    n = S // tk

    def scores(c, slot):
        start = pl.multiple_of(c * tk, tk)
        s = jnp.dot(k_ref[0, 0, pl.ds(start, tk), :], qT_ref[0, 0],
                    preferred_element_type=F32)
        s_ref[slot] = s
        mx_ref[slot] = jnp.max(s, axis=0, keepdims=True)

    def pv(c, slot):
        start = pl.multiple_of(c * tk, tk)
        return jnp.dot(vT_ref[0, :, pl.ds(start, tk)], p_ref[slot],
                       preferred_element_type=F32)

    def step(i, slot, prefetch):
        pv_prev = pv(jnp.maximum(i - 1, 0), 1 - slot)
        if prefetch:
            scores(i + 1, 1 - slot)
        m_old = m_ref[...]
        m_new = jnp.maximum(m_old, mx_ref[slot])
        alpha = jnp.exp2(m_old - m_new)
        p = jnp.exp2(s_ref[slot] - m_new)
        l_ref[...] = alpha * l_ref[...] + jnp.sum(p, axis=0, keepdims=True)
        p_ref[slot] = p.astype(BF16)
        m_ref[...] = m_new
        acc_ref[...] = (acc_ref[...] + pv_prev) * alpha

    m_ref[...] = jnp.full(m_ref.shape, NEG_BIG, F32)
    l_ref[...] = jnp.zeros(l_ref.shape, F32)
    acc_ref[...] = jnp.zeros(acc_ref.shape, F32)
    p_ref[1] = jnp.zeros(p_ref.shape[1:], BF16)
    scores(0, 0)

    def pair(j, carry):
        step(2 * j, 0, True)
        step(2 * j + 1, 1, True)
        return carry

    lax.fori_loop(0, n // 2 - 1, pair, 0)
    step(n - 2, 0, True)
    step(n - 1, 1, False)
    o = (acc_ref[...] + pv(n - 1, 1)) * (1.0 / l_ref[...])
    o_ref[0] = jnp.concatenate(
        [o[:, g * tq:(g + 1) * tq].T for g in range(GROUP)], axis=1).astype(BF16)


def _flash_call(qT, k, vT, tq, tk):
    B, _, S, _ = k.shape
    n_cols = GROUP * tq
    assert S % (2 * tk) == 0 and S // tk >= 2
    return pl.pallas_call(
        functools.partial(_flash_kernel, tk=tk, tq=tq),
        grid=(B, N_KV_HEADS, S // tq),
        in_specs=[
            pl.BlockSpec((1, 1, HEAD_DIM, n_cols), lambda b, j, i: (b, j, 0, i)),
            pl.BlockSpec((1, 1, S, HEAD_DIM), lambda b, j, i: (b, j, 0, 0)),
            pl.BlockSpec((1, HEAD_DIM, S), lambda b, j, i: (b, j, 0)),
        ],
        out_specs=pl.BlockSpec((1, tq, GROUP * HEAD_DIM), lambda b, j, i: (b, i, j)),
        out_shape=jax.ShapeDtypeStruct((B, S, N_HEADS * HEAD_DIM), BF16),
        scratch_shapes=[
            pltpu.VMEM((2, tk, n_cols), F32),
            pltpu.VMEM((2, tk, n_cols), BF16),
            pltpu.VMEM((2, 1, n_cols), F32),
            pltpu.VMEM((1, n_cols), F32),
            pltpu.VMEM((1, n_cols), F32),
            pltpu.VMEM((HEAD_DIM, n_cols), F32),
        ],
        compiler_params=pltpu.CompilerParams(
            dimension_semantics=("parallel", "parallel", "parallel"),
            vmem_limit_bytes=VMEM_LIMIT),
        name="flash_attn_t",
    )(qT, k, vT)


def _ffn(x, g_ref, wg_ref, wu_ref, wd_ref, act_ref):
    h = _rms(x, g_ref[...]).astype(BF16)
    f = wg_ref.shape[1]
    for c in range(f // MXU_N):
        cols = slice(c * MXU_N, (c + 1) * MXU_N)
        gate = jnp.dot(h, wg_ref[:, cols], preferred_element_type=F32)
        up = jnp.dot(h, wu_ref[:, cols], preferred_element_type=F32)
        act_ref[:, cols] = (gate * jax.nn.sigmoid(gate) * up).astype(BF16)
    return x + jnp.dot(act_ref[...], wd_ref[...], preferred_element_type=F32)


def _attn_out_ffn_kernel(x_ref, o_ref, wo_ref, g_ref, wg_ref, wu_ref, wd_ref,
                         y_ref, act_ref):
    x1 = x_ref[...] + jnp.dot(o_ref[...], wo_ref[...], preferred_element_type=F32)
    y_ref[...] = _ffn(x1, g_ref, wg_ref, wu_ref, wd_ref, act_ref)


def _attn_out_ffn_call(x, o, wo, g, wg, wu, wd, tm):
    T, D = x.shape
    f = wg.shape[1]
    row_spec = pl.BlockSpec((tm, D), lambda i: (i, 0))
    return pl.pallas_call(
        _attn_out_ffn_kernel,
        grid=(T // tm,),
        in_specs=[row_spec, row_spec, _const_spec((D, D)), _const_spec((1, D)),
                  _const_spec((D, f)), _const_spec((D, f)), _const_spec((f, D))],
        out_specs=row_spec,
        out_shape=jax.ShapeDtypeStruct((T, D), F32),
        scratch_shapes=[pltpu.VMEM((tm, f), BF16)],
        compiler_params=pltpu.CompilerParams(
            dimension_semantics=("parallel",),
            vmem_limit_bytes=VMEM_LIMIT),
        name="attn_out_ffn",
    )(x, o, wo, g, wg, wu, wd)


def _conv_ffn_kernel(xm_ref, xp_ref, xn_ref, cg_ref, win_ref, bin_ref, dww_ref, dwb_ref,
                     lng_ref, lnb_ref, wout_ref, bout_ref, fg_ref, wg_ref, wu_ref, wd_ref,
                     fin_ref, y_ref, u_ref, c_ref, act_ref, *, rc):
    ts, D = xm_ref.shape[1], xm_ref.shape[2]
    i = pl.program_id(1)
    n = pl.num_programs(1)
    xm = xm_ref[0]
    xe = jnp.concatenate([xp_ref[0], xm, xn_ref[0]], axis=0)
    h = _rms(xe, cg_ref[...]).astype(BF16)
    u = jnp.dot(h, win_ref[...], preferred_element_type=F32) + bin_ref[...]
    u = u[:, :D] * jax.nn.sigmoid(u[:, D:])
    row = lax.broadcasted_iota(jnp.int32, (ts + 2 * HALO, 1), 0)
    valid = jnp.logical_and(jnp.logical_or(row >= HALO, i > 0),
                            jnp.logical_or(row < ts + HALO, i < n - 1))
    u_ref[...] = jnp.where(valid, u, 0.0)
    off = HALO - CONV_PAD
    for r in range(ts // rc):
        for c in range(D // LANES):
            cols = slice(c * LANES, (c + 1) * LANES)
            acc = jnp.broadcast_to(dwb_ref[:, cols], (rc, LANES))
            for k in range(CONV_WIDTH):
                acc = acc + u_ref[pl.ds(r * rc + k + off, rc), cols] * dww_ref[k:k + 1, cols]
            c_ref[r * rc:(r + 1) * rc, cols] = acc
    v = c_ref[...]
    mu = jnp.mean(v, axis=-1, keepdims=True)
    vc = v - mu
    var = jnp.mean(vc * vc, axis=-1, keepdims=True)
    v = vc * lax.rsqrt(var + LN_EPS) * lng_ref[...] + lnb_ref[...]
    v = (v * jax.nn.sigmoid(v)).astype(BF16)
    x3 = xm + jnp.dot(v, wout_ref[...], preferred_element_type=F32) + bout_ref[...]
    x4 = _ffn(x3, fg_ref, wg_ref, wu_ref, wd_ref, act_ref)
    y_ref[0] = _rms(x4, fin_ref[...])


def _conv_ffn_call(x, cg, win, bin_, dww, dwb, lng, lnb, wout, bout, fg, wg, wu, wd, fin, ts, rc):
    B, S, D = x.shape
    f = wg.shape[1]
    hb = ts // HALO
    n_hb = S // HALO
    return pl.pallas_call(
        functools.partial(_conv_ffn_kernel, rc=rc),
        grid=(B, S // ts),
        in_specs=[
            pl.BlockSpec((1, ts, D), lambda b, i: (b, i, 0)),
            pl.BlockSpec((1, HALO, D), lambda b, i: (b, jnp.maximum(i * hb - 1, 0), 0)),
            pl.BlockSpec((1, HALO, D), lambda b, i: (b, jnp.minimum((i + 1) * hb, n_hb - 1), 0)),
            _const_spec((1, D)), _const_spec((D, 2 * D)), _const_spec((1, 2 * D)),
            _const_spec(dww.shape), _const_spec((1, D)), _const_spec((1, D)), _const_spec((1, D)),
            _const_spec((D, D)), _const_spec((1, D)), _const_spec((1, D)),
            _const_spec((D, f)), _const_spec((D, f)), _const_spec((f, D)), _const_spec((1, D)),
        ],
        out_specs=pl.BlockSpec((1, ts, D), lambda b, i: (b, i, 0)),
        out_shape=jax.ShapeDtypeStruct((B, S, D), F32),
        scratch_shapes=[
            pltpu.VMEM((ts + 2 * HALO, D), F32),
            pltpu.VMEM((ts, D), F32),
            pltpu.VMEM((ts, f), BF16),
        ],
        compiler_params=pltpu.CompilerParams(
            dimension_semantics=("parallel", "parallel"),
            vmem_limit_bytes=VMEM_LIMIT),
        name="conv_ffn_final",
    )(x, x, x, cg, win, bin_, dww, dwb, lng, lnb, wout, bout, fg, wg, wu, wd, fin)


def _rope_tables(seq_len):
    t = jnp.arange(seq_len, dtype=jnp.int32)
    row_ids = (t // GRID_W).astype(F32)
    col_ids = (t % GRID_W).astype(F32)
    inv_freq = ROPE_THETA ** (-jnp.arange(0, ROPE_AXIS_DIM, 2, dtype=F32) / ROPE_AXIS_DIM)
    ang_r = row_ids[:, None] * inv_freq[None, :]
    ang_c = col_ids[:, None] * inv_freq[None, :]
    zero = jnp.zeros_like(ang_r)
    cos = jnp.concatenate([jnp.cos(ang_r)] * 2 + [jnp.cos(ang_c)] * 2, axis=-1)
    sa = jnp.concatenate([zero, jnp.sin(ang_r), zero, jnp.sin(ang_c)], axis=-1)
    sb = jnp.concatenate([-jnp.sin(ang_r), zero, -jnp.sin(ang_c), zero], axis=-1)
    reps = LANES // HEAD_DIM
    return tuple(jnp.tile(a, (1, reps)) for a in (cos, sa, sb))


def _trunk(x, p, tables):
    B, S, D = x.shape
    cos, sa, sb = (t[:S] for t in tables)
    qT, k, vT = _qkv_call(x, p["attn_g"], p["w_qkv"], p["gqk"], p["seg"], cos, sa, sb,
                          tm=512, tq=256)
    o = _flash_call(qT, k, vT, tq=256, tk=256)
    x2 = _attn_out_ffn_call(x.reshape(B * S, D), o.reshape(B * S, D), p["w_o"], p["ffn_g0"],
                            p["wg0"], p["wu0"], p["wd0"], tm=512)
    return _conv_ffn_call(x2.reshape(B, S, D), p["conv_g"], p["w_in"], p["b_in"], p["dw_w"],
                          p["dw_b"], p["ln_g"], p["ln_b"], p["w_out"], p["b_out"], p["ffn_g1"],
                          p["wg1"], p["wu1"], p["wd1"], p["fin_g"], ts=512, rc=64)


def kernel(x_prompt, x_sample, attn_norm_g, w_qkv, q_norm_g, k_norm_g, w_o, conv_norm_g, conv_w_in, conv_b_in, dw_w, dw_b, conv_ln_g, conv_ln_b, conv_w_out, conv_b_out, ffn_norm_g, w_gate, w_up, w_down, final_norm_g):
    row = lambda v: v.reshape(1, -1).astype(F32)
    scale = math.log2(math.e) / math.sqrt(HEAD_DIM)
    gqk = jnp.concatenate([jnp.tile(q_norm_g[0] * scale, N_HEADS),
                           jnp.tile(k_norm_g[0], N_KV_HEADS)]).reshape(1, -1)
    lane = jnp.arange(MXU_N) // HEAD_DIM
    seg = jnp.where(lane[:, None] == lane[None, :], 1.0 / HEAD_DIM, 0.0).astype(BF16)
    p = dict(
        attn_g=row(attn_norm_g[0]), w_qkv=w_qkv[0].astype(BF16), gqk=gqk, seg=seg,
        w_o=w_o[0].astype(BF16), ffn_g0=row(ffn_norm_g[0]),
        wg0=w_gate[0].astype(BF16), wu0=w_up[0].astype(BF16), wd0=w_down[0].astype(BF16),
        conv_g=row(conv_norm_g[0]), w_in=conv_w_in[0].astype(BF16), b_in=row(conv_b_in[0]),
        dw_w=jnp.pad(dw_w[0], ((0, 1), (0, 0))), dw_b=row(dw_b[0]),
        ln_g=row(conv_ln_g[0]), ln_b=row(conv_ln_b[0]),
        w_out=conv_w_out[0].astype(BF16), b_out=row(conv_b_out[0]), ffn_g1=row(ffn_norm_g[1]),
        wg1=w_gate[1].astype(BF16), wu1=w_up[1].astype(BF16), wd1=w_down[1].astype(BF16),
        fin_g=row(final_norm_g),
    )
    tables = _rope_tables(max(x_prompt.shape[1], x_sample.shape[1]))
    return (_trunk(x_prompt, p, tables), _trunk(x_sample, p, tables))
```

```python
import functools
import math

import jax
import jax.numpy as jnp
from jax import lax
from jax.experimental import pallas as pl
from jax.experimental.pallas import tpu as pltpu

HEAD_DIM = 64
N_HEADS = 16
N_KV_HEADS = 4
GROUP = N_HEADS // N_KV_HEADS
GRID_W = 64
ROPE_AXIS_DIM = HEAD_DIM // 2
ROPE_HALF = ROPE_AXIS_DIM // 2
ROPE_THETA = 10000.0
CONV_WIDTH = 31
CONV_PAD = CONV_WIDTH // 2
RMS_EPS = 1e-6
LN_EPS = 1e-5

LANES = 128
MXU_N = 256
HALO = 16
BF16_SUBLANES = 16
V_ROWS = HEAD_DIM + BF16_SUBLANES
VMEM_LIMIT = 56 * 1024 * 1024

F32 = jnp.float32
BF16 = jnp.bfloat16


def _const_spec(shape):
    zeros = (0,) * len(shape)
    return pl.BlockSpec(shape, lambda *_: zeros, pipeline_mode=pl.Buffered(1))


def _rms(x, g):
    ms = jnp.mean(x * x, axis=-1, keepdims=True)
    return x * lax.rsqrt(ms + RMS_EPS) * g


def _qkv_kernel(x_ref, g_ref, w_ref, gqk_ref, seg_ref, cos_ref, sa_ref, sb_ref,
                qT_ref, k_ref, vT_ref, *, tq):
    tm = x_ref.shape[1]
    q_dim = N_HEADS * HEAD_DIM
    qk_dim = q_dim + N_KV_HEADS * HEAD_DIM
    h = _rms(x_ref[0], g_ref[...]).astype(BF16)
    qkv = jnp.dot(h, w_ref[...], preferred_element_type=F32)
    seg = seg_ref[...]
    cos, sa, sb = cos_ref[...], sa_ref[...], sb_ref[...]
    for c in range(qk_dim // MXU_N):
        z = qkv[:, c * MXU_N:(c + 1) * MXU_N]
        y = z * z
        y_hi = y.astype(BF16)
        y_lo = (y - y_hi.astype(F32)).astype(BF16)
        ms = (jnp.dot(y_hi, seg, preferred_element_type=F32)
              + jnp.dot(y_lo, seg, preferred_element_type=F32))
        zn = z * lax.rsqrt(ms + RMS_EPS) * gqk_ref[:, c * MXU_N:(c + 1) * MXU_N]
        for half in range(MXU_N // LANES):
            zz = zn[:, half * LANES:(half + 1) * LANES]
            r = (zz * cos
                 + pltpu.roll(zz, ROPE_HALF, 1) * sa
                 + pltpu.roll(zz, LANES - ROPE_HALF, 1) * sb)
            col = c * MXU_N + half * LANES
            if col < q_dim:
                rT = r.T.astype(BF16)
                for hh in range(LANES // HEAD_DIM):
                    g = half * (LANES // HEAD_DIM) + hh
                    for tt in range(tm // tq):
                        dst = (tt * GROUP + g) * tq
                        qT_ref[0, c, :, dst:dst + tq] = rT[hh * HEAD_DIM:(hh + 1) * HEAD_DIM,
                                                           tt * tq:(tt + 1) * tq]
            else:
                kv = (col - q_dim) // HEAD_DIM
                k_ref[0, kv] = r[:, :HEAD_DIM].astype(BF16)
                k_ref[0, kv + 1] = r[:, HEAD_DIM:].astype(BF16)
    vT = qkv[:, qk_dim:].T.astype(BF16)
    ones_rows = (lax.broadcasted_iota(jnp.int32, (V_ROWS - HEAD_DIM, tm), 0) == 0).astype(BF16)
    for kv in range(N_KV_HEADS):
        vT_ref[0, kv, :HEAD_DIM, :] = vT[kv * HEAD_DIM:(kv + 1) * HEAD_DIM]
        vT_ref[0, kv, HEAD_DIM:, :] = ones_rows


def _qkv_call(x, g, w_qkv, gqk, seg, cos, sa, sb, tm, tq):
    B, S, D = x.shape
    n_s = S // tm
    qkv_dim = w_qkv.shape[1]
    kv_dim = N_KV_HEADS * HEAD_DIM
    tab_spec = pl.BlockSpec((tm, LANES), lambda b, i: (i, 0))
    return pl.pallas_call(
        functools.partial(_qkv_kernel, tq=tq),
        grid=(B, n_s),
        in_specs=[
            pl.BlockSpec((1, tm, D), lambda b, i: (b, i, 0)),
            _const_spec((1, D)),
            _const_spec((D, qkv_dim)),
            _const_spec((1, gqk.shape[1])),
            _const_spec((MXU_N, MXU_N)),
            tab_spec, tab_spec, tab_spec,
        ],
        out_specs=[
            pl.BlockSpec((1, N_KV_HEADS, HEAD_DIM, GROUP * tm), lambda b, i: (b, 0, 0, i)),
            pl.BlockSpec((1, N_KV_HEADS, tm, HEAD_DIM), lambda b, i: (b, 0, i, 0)),
            pl.BlockSpec((1, N_KV_HEADS, V_ROWS, tm), lambda b, i: (b, 0, 0, i)),
        ],
        out_shape=[
            jax.ShapeDtypeStruct((B, N_KV_HEADS, HEAD_DIM, GROUP * S), BF16),
            jax.ShapeDtypeStruct((B, N_KV_HEADS, S, HEAD_DIM), BF16),
            jax.ShapeDtypeStruct((B, N_KV_HEADS, V_ROWS, S), BF16),
        ],
        compiler_params=pltpu.CompilerParams(
            dimension_semantics=("parallel", "parallel"),
            vmem_limit_bytes=VMEM_LIMIT),
        name="qkv_rope",
    )(x, g, w_qkv, gqk, seg, cos, sa, sb)


NEG_BIG = -1e30


S_SLOTS = 4
P_SLOTS = 2
LOOKAHEAD = 2
UNROLL = 8


def _flash_kernel(qT_ref, k_ref, vT_ref, o_ref, s_ref, p_ref, mx_ref, m_ref, acc_ref,
                  *, tk, tq):
    S = k_ref.shape[2]
    n = S // tk

    def scores(c, c_static):
        start = pl.multiple_of(c * tk, tk)
        s = jnp.dot(k_ref[0, 0, pl.ds(start, tk), :], qT_ref[0, 0],
                    preferred_element_type=F32)
        s_ref[c_static % S_SLOTS] = s
        mx_ref[c_static % S_SLOTS] = jnp.max(s, axis=0, keepdims=True)

    def pv(c, c_static):
        start = pl.multiple_of(c * tk, tk)
        return jnp.dot(vT_ref[0, 0, :, pl.ds(start, tk)], p_ref[c_static % P_SLOTS],
                       preferred_element_type=F32)

    def step(base, u, prefetch):
        i = base + u
        if prefetch:
            scores(i + LOOKAHEAD, u + LOOKAHEAD)
        pv_prev = pv(jnp.maximum(i - 1, 0), u - 1)
        m_old = m_ref[...]
        m_new = jnp.maximum(m_old, mx_ref[u % S_SLOTS])
        alpha = jnp.exp2(m_old - m_new)
        p_ref[u % P_SLOTS] = jnp.exp2(s_ref[u % S_SLOTS] - m_new).astype(BF16)
        m_ref[...] = m_new
        acc_ref[...] = (acc_ref[...] + pv_prev) * alpha

    m_ref[...] = jnp.full(m_ref.shape, NEG_BIG, F32)
    acc_ref[...] = jnp.zeros(acc_ref.shape, F32)
    p_ref[(-1) % P_SLOTS] = jnp.zeros(p_ref.shape[1:], BF16)
    for c in range(LOOKAHEAD):
        scores(c, c)

    def group(j, carry):
        for u in range(UNROLL):
            step(j * UNROLL, u, True)
        return carry

    lax.fori_loop(0, n // UNROLL - 1, group, 0)
    for u in range(UNROLL):
        step(n - UNROLL, u, u + LOOKAHEAD < UNROLL)
    acc = acc_ref[...] + pv(n - 1, n - 1)
    o = acc[:HEAD_DIM] * (1.0 / acc[HEAD_DIM:HEAD_DIM + 1])
    o_ref[0] = jnp.concatenate(
        [o[:, g * tq:(g + 1) * tq].T for g in range(GROUP)], axis=1).astype(BF16)


def _flash_call(qT, k, vT, tq, tk):
    B, _, S, _ = k.shape
    n_cols = GROUP * tq
    assert S % (UNROLL * tk) == 0 and UNROLL % S_SLOTS == 0 and UNROLL % P_SLOTS == 0
    assert LOOKAHEAD < S_SLOTS
    return pl.pallas_call(
        functools.partial(_flash_kernel, tk=tk, tq=tq),
        grid=(B, N_KV_HEADS, S // tq),
        in_specs=[
            pl.BlockSpec((1, 1, HEAD_DIM, n_cols), lambda b, j, i: (b, j, 0, i)),
            pl.BlockSpec((1, 1, S, HEAD_DIM), lambda b, j, i: (b, j, 0, 0)),
            pl.BlockSpec((1, 1, V_ROWS, S), lambda b, j, i: (b, j, 0, 0)),
        ],
        out_specs=pl.BlockSpec((1, tq, GROUP * HEAD_DIM), lambda b, j, i: (b, i, j)),
        out_shape=jax.ShapeDtypeStruct((B, S, N_HEADS * HEAD_DIM), BF16),
        scratch_shapes=[
            pltpu.VMEM((S_SLOTS, tk, n_cols), F32),
            pltpu.VMEM((P_SLOTS, tk, n_cols), BF16),
            pltpu.VMEM((S_SLOTS, 1, n_cols), F32),
            pltpu.VMEM((1, n_cols), F32),
            pltpu.VMEM((V_ROWS, n_cols), F32),
        ],
        compiler_params=pltpu.CompilerParams(
            dimension_semantics=("parallel", "parallel", "parallel"),
            vmem_limit_bytes=VMEM_LIMIT),
        name="flash_attn_t",
    )(qT, k, vT)


def _ffn(x, g_ref, wg_ref, wu_ref, wd_ref, act_ref):
    h = _rms(x, g_ref[...]).astype(BF16)
    f = wg_ref.shape[1]
    for c in range(f // MXU_N):
        cols = slice(c * MXU_N, (c + 1) * MXU_N)
        gate = jnp.dot(h, wg_ref[:, cols], preferred_element_type=F32)
        up = jnp.dot(h, wu_ref[:, cols], preferred_element_type=F32)
        act_ref[:, cols] = (gate * jax.nn.sigmoid(gate) * up).astype(BF16)
    return x + jnp.dot(act_ref[...], wd_ref[...], preferred_element_type=F32)


def _attn_out_ffn_kernel(x_ref, o_ref, wo_ref, g_ref, wg_ref, wu_ref, wd_ref,
                         y_ref, act_ref):
    x1 = x_ref[...] + jnp.dot(o_ref[...], wo_ref[...], preferred_element_type=F32)
    y_ref[...] = _ffn(x1, g_ref, wg_ref, wu_ref, wd_ref, act_ref)


def _attn_out_ffn_call(x, o, wo, g, wg, wu, wd, tm):
    T, D = x.shape
    f = wg.shape[1]
    row_spec = pl.BlockSpec((tm, D), lambda i: (i, 0))
    return pl.pallas_call(
        _attn_out_ffn_kernel,
        grid=(T // tm,),
        in_specs=[row_spec, row_spec, _const_spec((D, D)), _const_spec((1, D)),
                  _const_spec((D, f)), _const_spec((D, f)), _const_spec((f, D))],
        out_specs=row_spec,
        out_shape=jax.ShapeDtypeStruct((T, D), F32),
        scratch_shapes=[pltpu.VMEM((tm, f), BF16)],
        compiler_params=pltpu.CompilerParams(
            dimension_semantics=("parallel",),
            vmem_limit_bytes=VMEM_LIMIT),
        name="attn_out_ffn",
    )(x, o, wo, g, wg, wu, wd)


def _conv_ffn_kernel(xm_ref, xp_ref, xn_ref, cg_ref, win_ref, bin_ref, dww_ref, dwb_ref,
                     lng_ref, lnb_ref, wout_ref, bout_ref, fg_ref, wg_ref, wu_ref, wd_ref,
                     fin_ref, y_ref, u_ref, c_ref, act_ref, *, rc):
    ts, D = xm_ref.shape[1], xm_ref.shape[2]
    i = pl.program_id(1)
    n = pl.num_programs(1)
    xm = xm_ref[0]
    xe = jnp.concatenate([xp_ref[0], xm, xn_ref[0]], axis=0)
    h = _rms(xe, cg_ref[...]).astype(BF16)
    u = jnp.dot(h, win_ref[...], preferred_element_type=F32) + bin_ref[...]
    u = u[:, :D] * jax.nn.sigmoid(u[:, D:])
    row = lax.broadcasted_iota(jnp.int32, (ts + 2 * HALO, 1), 0)
    valid = jnp.logical_and(jnp.logical_or(row >= HALO, i > 0),
                            jnp.logical_or(row < ts + HALO, i < n - 1))
    u_ref[...] = jnp.where(valid, u, 0.0)
    off = HALO - CONV_PAD
    for r in range(ts // rc):
        for c in range(D // LANES):
            cols = slice(c * LANES, (c + 1) * LANES)
            acc = jnp.broadcast_to(dwb_ref[:, cols], (rc, LANES))
            for k in range(CONV_WIDTH):
                acc = acc + u_ref[pl.ds(r * rc + k + off, rc), cols] * dww_ref[k:k + 1, cols]
            c_ref[r * rc:(r + 1) * rc, cols] = acc
    v = c_ref[...]
    mu = jnp.mean(v, axis=-1, keepdims=True)
    vc = v - mu
    var = jnp.mean(vc * vc, axis=-1, keepdims=True)
    v = vc * lax.rsqrt(var + LN_EPS) * lng_ref[...] + lnb_ref[...]
    v = (v * jax.nn.sigmoid(v)).astype(BF16)
    x3 = xm + jnp.dot(v, wout_ref[...], preferred_element_type=F32) + bout_ref[...]
    x4 = _ffn(x3, fg_ref, wg_ref, wu_ref, wd_ref, act_ref)
    y_ref[0] = _rms(x4, fin_ref[...])


def _conv_ffn_call(x, cg, win, bin_, dww, dwb, lng, lnb, wout, bout, fg, wg, wu, wd, fin, ts, rc):
    B, S, D = x.shape
    f = wg.shape[1]
    hb = ts // HALO
    n_hb = S // HALO
    return pl.pallas_call(
        functools.partial(_conv_ffn_kernel, rc=rc),
        grid=(B, S // ts),
        in_specs=[
            pl.BlockSpec((1, ts, D), lambda b, i: (b, i, 0)),
            pl.BlockSpec((1, HALO, D), lambda b, i: (b, jnp.maximum(i * hb - 1, 0), 0)),
            pl.BlockSpec((1, HALO, D), lambda b, i: (b, jnp.minimum((i + 1) * hb, n_hb - 1), 0)),
            _const_spec((1, D)), _const_spec((D, 2 * D)), _const_spec((1, 2 * D)),
            _const_spec(dww.shape), _const_spec((1, D)), _const_spec((1, D)), _const_spec((1, D)),
            _const_spec((D, D)), _const_spec((1, D)), _const_spec((1, D)),
            _const_spec((D, f)), _const_spec((D, f)), _const_spec((f, D)), _const_spec((1, D)),
        ],
        out_specs=pl.BlockSpec((1, ts, D), lambda b, i: (b, i, 0)),
        out_shape=jax.ShapeDtypeStruct((B, S, D), F32),
        scratch_shapes=[
            pltpu.VMEM((ts + 2 * HALO, D), F32),
            pltpu.VMEM((ts, D), F32),
            pltpu.VMEM((ts, f), BF16),
        ],
        compiler_params=pltpu.CompilerParams(
            dimension_semantics=("parallel", "parallel"),
            vmem_limit_bytes=VMEM_LIMIT),
        name="conv_ffn_final",
    )(x, x, x, cg, win, bin_, dww, dwb, lng, lnb, wout, bout, fg, wg, wu, wd, fin)


def _rope_tables(seq_len):
    t = jnp.arange(seq_len, dtype=jnp.int32)
    row_ids = (t // GRID_W).astype(F32)
    col_ids = (t % GRID_W).astype(F32)
    inv_freq = ROPE_THETA ** (-jnp.arange(0, ROPE_AXIS_DIM, 2, dtype=F32) / ROPE_AXIS_DIM)
    ang_r = row_ids[:, None] * inv_freq[None, :]
    ang_c = col_ids[:, None] * inv_freq[None, :]
    zero = jnp.zeros_like(ang_r)
    cos = jnp.concatenate([jnp.cos(ang_r)] * 2 + [jnp.cos(ang_c)] * 2, axis=-1)
    sa = jnp.concatenate([zero, jnp.sin(ang_r), zero, jnp.sin(ang_c)], axis=-1)
    sb = jnp.concatenate([-jnp.sin(ang_r), zero, -jnp.sin(ang_c), zero], axis=-1)
    reps = LANES // HEAD_DIM
    return tuple(jnp.tile(a, (1, reps)) for a in (cos, sa, sb))


def _trunk(x, p, tables):
    B, S, D = x.shape
    cos, sa, sb = (t[:S] for t in tables)
    qT, k, vT = _qkv_call(x, p["attn_g"], p["w_qkv"], p["gqk"], p["seg"], cos, sa, sb,
                          tm=512, tq=256)
    o = _flash_call(qT, k, vT, tq=256, tk=256)
    x2 = _attn_out_ffn_call(x.reshape(B * S, D), o.reshape(B * S, D), p["w_o"], p["ffn_g0"],
                            p["wg0"], p["wu0"], p["wd0"], tm=512)
    return _conv_ffn_call(x2.reshape(B, S, D), p["conv_g"], p["w_in"], p["b_in"], p["dw_w"],
                          p["dw_b"], p["ln_g"], p["ln_b"], p["w_out"], p["b_out"], p["ffn_g1"],
                          p["wg1"], p["wu1"], p["wd1"], p["fin_g"], ts=512, rc=64)


def kernel(x_prompt, x_sample, attn_norm_g, w_qkv, q_norm_g, k_norm_g, w_o, conv_norm_g, conv_w_in, conv_b_in, dw_w, dw_b, conv_ln_g, conv_ln_b, conv_w_out, conv_b_out, ffn_norm_g, w_gate, w_up, w_down, final_norm_g):
    row = lambda v: v.reshape(1, -1).astype(F32)
    scale = math.log2(math.e) / math.sqrt(HEAD_DIM)
    gqk = jnp.concatenate([jnp.tile(q_norm_g[0] * scale, N_HEADS),
                           jnp.tile(k_norm_g[0], N_KV_HEADS)]).reshape(1, -1)
    lane = jnp.arange(MXU_N) // HEAD_DIM
    seg = jnp.where(lane[:, None] == lane[None, :], 1.0 / HEAD_DIM, 0.0).astype(BF16)
    p = dict(
        attn_g=row(attn_norm_g[0]), w_qkv=w_qkv[0].astype(BF16), gqk=gqk, seg=seg,
        w_o=w_o[0].astype(BF16), ffn_g0=row(ffn_norm_g[0]),
        wg0=w_gate[0].astype(BF16), wu0=w_up[0].astype(BF16), wd0=w_down[0].astype(BF16),
        conv_g=row(conv_norm_g[0]), w_in=conv_w_in[0].astype(BF16), b_in=row(conv_b_in[0]),
        dw_w=jnp.pad(dw_w[0], ((0, 1), (0, 0))), dw_b=row(dw_b[0]),
        ln_g=row(conv_ln_g[0]), ln_b=row(conv_ln_b[0]),
        w_out=conv_w_out[0].astype(BF16), b_out=row(conv_b_out[0]), ffn_g1=row(ffn_norm_g[1]),
        wg1=w_gate[1].astype(BF16), wu1=w_up[1].astype(BF16), wd1=w_down[1].astype(BF16),
        fin_g=row(final_norm_g),
    )
    tables = _rope_tables(max(x_prompt.shape[1], x_sample.shape[1]))
    return (_trunk(x_prompt, p, tables), _trunk(x_sample, p, tables))
```

```python
import functools
import math

import jax
import jax.numpy as jnp
from jax import lax
from jax.experimental import pallas as pl
from jax.experimental.pallas import tpu as pltpu

HEAD_DIM = 64
N_HEADS = 16
N_KV_HEADS = 4
GROUP = N_HEADS // N_KV_HEADS
GRID_W = 64
ROPE_AXIS_DIM = HEAD_DIM // 2
ROPE_HALF = ROPE_AXIS_DIM // 2
ROPE_THETA = 10000.0
CONV_WIDTH = 31
CONV_PAD = CONV_WIDTH // 2
RMS_EPS = 1e-6
LN_EPS = 1e-5

LANES = 128
SUBLANES = 8
MXU_N = 256
HALO = 16
BF16_SUBLANES = 16
V_ROWS = HEAD_DIM + BF16_SUBLANES
VMEM_LIMIT = 56 * 1024 * 1024

F32 = jnp.float32
BF16 = jnp.bfloat16


def _const_spec(shape):
    zeros = (0,) * len(shape)
    return pl.BlockSpec(shape, lambda *_: zeros, pipeline_mode=pl.Buffered(1))


def _rms(x, g):
    ms = jnp.mean(x * x, axis=-1, keepdims=True)
    return x * lax.rsqrt(ms + RMS_EPS) * g


def _qkv_kernel(x_ref, g_ref, w_ref, gqk_ref, seg_ref, cos_ref, sa_ref, sb_ref,
                qT_ref, k_ref, vT_ref, *, tq):
    tm = x_ref.shape[1]
    q_dim = N_HEADS * HEAD_DIM
    qk_dim = q_dim + N_KV_HEADS * HEAD_DIM
    h = _rms(x_ref[0], g_ref[...]).astype(BF16)
    qkv = jnp.dot(h, w_ref[...], preferred_element_type=F32)
    seg = seg_ref[...]
    cos, sa, sb = cos_ref[...], sa_ref[...], sb_ref[...]
    for c in range(qk_dim // MXU_N):
        z = qkv[:, c * MXU_N:(c + 1) * MXU_N]
        y = z * z
        y_hi = y.astype(BF16)
        y_lo = (y - y_hi.astype(F32)).astype(BF16)
        ms = (jnp.dot(y_hi, seg, preferred_element_type=F32)
              + jnp.dot(y_lo, seg, preferred_element_type=F32))
        zn = z * lax.rsqrt(ms + RMS_EPS) * gqk_ref[:, c * MXU_N:(c + 1) * MXU_N]
        for half in range(MXU_N // LANES):
            zz = zn[:, half * LANES:(half + 1) * LANES]
            r = (zz * cos
                 + pltpu.roll(zz, ROPE_HALF, 1) * sa
                 + pltpu.roll(zz, LANES - ROPE_HALF, 1) * sb)
            col = c * MXU_N + half * LANES
            if col < q_dim:
                rT = r.T.astype(BF16)
                for hh in range(LANES // HEAD_DIM):
                    g = half * (LANES // HEAD_DIM) + hh
                    for tt in range(tm // tq):
                        dst = (tt * GROUP + g) * tq
                        qT_ref[0, c, :, dst:dst + tq] = rT[hh * HEAD_DIM:(hh + 1) * HEAD_DIM,
                                                           tt * tq:(tt + 1) * tq]
            else:
                kv = (col - q_dim) // HEAD_DIM
                k_ref[0, kv] = r[:, :HEAD_DIM].astype(BF16)
                k_ref[0, kv + 1] = r[:, HEAD_DIM:].astype(BF16)
    vT = qkv[:, qk_dim:].T.astype(BF16)
    ones_rows = (lax.broadcasted_iota(jnp.int32, (V_ROWS - HEAD_DIM, tm), 0) == 0).astype(BF16)
    for kv in range(N_KV_HEADS):
        vT_ref[0, kv, :HEAD_DIM, :] = vT[kv * HEAD_DIM:(kv + 1) * HEAD_DIM]
        vT_ref[0, kv, HEAD_DIM:, :] = ones_rows


def _qkv_call(x, g, w_qkv, gqk, seg, cos, sa, sb, tm, tq):
    B, S, D = x.shape
    n_s = S // tm
    qkv_dim = w_qkv.shape[1]
    kv_dim = N_KV_HEADS * HEAD_DIM
    tab_spec = pl.BlockSpec((tm, LANES), lambda b, i: (i, 0))
    return pl.pallas_call(
        functools.partial(_qkv_kernel, tq=tq),
        grid=(B, n_s),
        in_specs=[
            pl.BlockSpec((1, tm, D), lambda b, i: (b, i, 0)),
            _const_spec((1, D)),
            _const_spec((D, qkv_dim)),
            _const_spec((1, gqk.shape[1])),
            _const_spec((MXU_N, MXU_N)),
            tab_spec, tab_spec, tab_spec,
        ],
        out_specs=[
            pl.BlockSpec((1, N_KV_HEADS, HEAD_DIM, GROUP * tm), lambda b, i: (b, 0, 0, i)),
            pl.BlockSpec((1, N_KV_HEADS, tm, HEAD_DIM), lambda b, i: (b, 0, i, 0)),
            pl.BlockSpec((1, N_KV_HEADS, V_ROWS, tm), lambda b, i: (b, 0, 0, i)),
        ],
        out_shape=[
            jax.ShapeDtypeStruct((B, N_KV_HEADS, HEAD_DIM, GROUP * S), BF16),
            jax.ShapeDtypeStruct((B, N_KV_HEADS, S, HEAD_DIM), BF16),
            jax.ShapeDtypeStruct((B, N_KV_HEADS, V_ROWS, S), BF16),
        ],
        compiler_params=pltpu.CompilerParams(
            dimension_semantics=("parallel", "parallel"),
            vmem_limit_bytes=VMEM_LIMIT),
        name="qkv_rope",
    )(x, g, w_qkv, gqk, seg, cos, sa, sb)


NEG_BIG = -1e30
SCORE_BOUND_MAX = 60.0
BF16_ROUNDING_MARGIN = 1.02


S_SLOTS = 4
P_SLOTS = 2
LOOKAHEAD = 2
UNROLL = 8


def _flash_kernel(qT_ref, k_ref, vT_ref, o_ref, s_ref, p_ref, mx_ref, m_ref, acc_ref,
                  *, tk, tq):
    S = k_ref.shape[2]
    n = S // tk

    def scores(c, c_static):
        start = pl.multiple_of(c * tk, tk)
        s = jnp.dot(k_ref[0, 0, pl.ds(start, tk), :], qT_ref[0, 0],
                    preferred_element_type=F32)
        s_ref[c_static % S_SLOTS] = s
        mx_ref[c_static % S_SLOTS] = jnp.max(s, axis=0, keepdims=True)

    def pv(c, c_static):
        start = pl.multiple_of(c * tk, tk)
        return jnp.dot(vT_ref[0, 0, :, pl.ds(start, tk)], p_ref[c_static % P_SLOTS],
                       preferred_element_type=F32)

    def step(base, u, prefetch):
        i = base + u
        if prefetch:
            scores(i + LOOKAHEAD, u + LOOKAHEAD)
        pv_prev = pv(jnp.maximum(i - 1, 0), u - 1)
        m_old = m_ref[...]
        m_new = jnp.maximum(m_old, mx_ref[u % S_SLOTS])
        alpha = jnp.exp2(m_old - m_new)
        p_ref[u % P_SLOTS] = jnp.exp2(s_ref[u % S_SLOTS] - m_new).astype(BF16)
        m_ref[...] = m_new
        acc_ref[...] = (acc_ref[...] + pv_prev) * alpha

    m_ref[...] = jnp.full(m_ref.shape, NEG_BIG, F32)
    acc_ref[...] = jnp.zeros(acc_ref.shape, F32)
    p_ref[(-1) % P_SLOTS] = jnp.zeros(p_ref.shape[1:], BF16)
    for c in range(LOOKAHEAD):
        scores(c, c)

    def group(j, carry):
        for u in range(UNROLL):
            step(j * UNROLL, u, True)
        return carry

    lax.fori_loop(0, n // UNROLL - 1, group, 0)
    for u in range(UNROLL):
        step(n - UNROLL, u, u + LOOKAHEAD < UNROLL)
    _flash_finish(acc_ref[...] + pv(n - 1, n - 1), o_ref, tq)


def _flash_finish(acc, o_ref, tq):
    o = acc[:HEAD_DIM] * (1.0 / acc[HEAD_DIM:HEAD_DIM + 1])
    o_ref[0] = jnp.concatenate(
        [o[:, g * tq:(g + 1) * tq].T for g in range(GROUP)], axis=1).astype(BF16)


def _flash_bounded_kernel(qT_ref, k_ref, vT_ref, o_ref, p_ref, acc_ref, *, tk, tq):
    S = k_ref.shape[2]
    n = S // tk

    def numerators(c, c_static):
        start = pl.multiple_of(c * tk, tk)
        s = jnp.dot(k_ref[0, 0, pl.ds(start, tk), :], qT_ref[0, 0],
                    preferred_element_type=F32)
        p_ref[c_static % S_SLOTS] = jnp.exp2(s).astype(BF16)

    def step(base, u, prefetch):
        i = base + u
        if prefetch:
            numerators(i + LOOKAHEAD, u + LOOKAHEAD)
        start = pl.multiple_of(i * tk, tk)
        acc_ref[...] += jnp.dot(vT_ref[0, 0, :, pl.ds(start, tk)], p_ref[u % S_SLOTS],
                                preferred_element_type=F32)

    acc_ref[...] = jnp.zeros(acc_ref.shape, F32)
    for c in range(LOOKAHEAD):
        numerators(c, c)

    def group(j, carry):
        for u in range(UNROLL):
            step(j * UNROLL, u, True)
        return carry

    lax.fori_loop(0, n // UNROLL - 1, group, 0)
    for u in range(UNROLL):
        step(n - UNROLL, u, u + LOOKAHEAD < UNROLL)
    _flash_finish(acc_ref[...], o_ref, tq)


def _flash_call(qT, k, vT, tq, tk, online):
    B, _, S, _ = k.shape
    n_cols = GROUP * tq
    assert S % (UNROLL * tk) == 0 and UNROLL % S_SLOTS == 0 and UNROLL % P_SLOTS == 0
    assert LOOKAHEAD < S_SLOTS
    acc_scratch = pltpu.VMEM((V_ROWS, n_cols), F32)
    if online:
        body = _flash_kernel
        scratch = [
            pltpu.VMEM((S_SLOTS, tk, n_cols), F32),
            pltpu.VMEM((P_SLOTS, tk, n_cols), BF16),
            pltpu.VMEM((S_SLOTS, 1, n_cols), F32),
            pltpu.VMEM((1, n_cols), F32),
            acc_scratch,
        ]
    else:
        body = _flash_bounded_kernel
        scratch = [pltpu.VMEM((S_SLOTS, tk, n_cols), BF16), acc_scratch]
    return pl.pallas_call(
        functools.partial(body, tk=tk, tq=tq),
        grid=(B, N_KV_HEADS, S // tq),
        in_specs=[
            pl.BlockSpec((1, 1, HEAD_DIM, n_cols), lambda b, j, i: (b, j, 0, i)),
            pl.BlockSpec((1, 1, S, HEAD_DIM), lambda b, j, i: (b, j, 0, 0)),
            pl.BlockSpec((1, 1, V_ROWS, S), lambda b, j, i: (b, j, 0, 0)),
        ],
        out_specs=pl.BlockSpec((1, tq, GROUP * HEAD_DIM), lambda b, j, i: (b, i, j)),
        out_shape=jax.ShapeDtypeStruct((B, S, N_HEADS * HEAD_DIM), BF16),
        scratch_shapes=scratch,
        compiler_params=pltpu.CompilerParams(
            dimension_semantics=("parallel", "parallel", "parallel"),
            vmem_limit_bytes=VMEM_LIMIT),
        name="flash_online" if online else "flash_bounded",
    )(qT, k, vT)


def _ffn(x, g_ref, wg_ref, wu_ref, wd_ref, act_ref):
    h = _rms(x, g_ref[...]).astype(BF16)
    f = wg_ref.shape[1]
    for c in range(f // MXU_N):
        cols = slice(c * MXU_N, (c + 1) * MXU_N)
        gate = jnp.dot(h, wg_ref[:, cols], preferred_element_type=F32)
        up = jnp.dot(h, wu_ref[:, cols], preferred_element_type=F32)
        act_ref[:, cols] = (gate * jax.nn.sigmoid(gate) * up).astype(BF16)
    return x + jnp.dot(act_ref[...], wd_ref[...], preferred_element_type=F32)


def _attn_out_ffn_kernel(x_ref, o_ref, wo_ref, g_ref, wg_ref, wu_ref, wd_ref,
                         y_ref, act_ref):
    x1 = x_ref[...] + jnp.dot(o_ref[...], wo_ref[...], preferred_element_type=F32)
    y_ref[...] = _ffn(x1, g_ref, wg_ref, wu_ref, wd_ref, act_ref)


def _attn_out_ffn_call(x, o, wo, g, wg, wu, wd, tm):
    T, D = x.shape
    f = wg.shape[1]
    row_spec = pl.BlockSpec((tm, D), lambda i: (i, 0))
    return pl.pallas_call(
        _attn_out_ffn_kernel,
        grid=(T // tm,),
        in_specs=[row_spec, row_spec, _const_spec((D, D)), _const_spec((1, D)),
                  _const_spec((D, f)), _const_spec((D, f)), _const_spec((f, D))],
        out_specs=row_spec,
        out_shape=jax.ShapeDtypeStruct((T, D), F32),
        scratch_shapes=[pltpu.VMEM((tm, f), BF16)],
        compiler_params=pltpu.CompilerParams(
            dimension_semantics=("parallel",),
            vmem_limit_bytes=VMEM_LIMIT),
        name="attn_out_ffn",
    )(x, o, wo, g, wg, wu, wd)


def _conv_ffn_kernel(xm_ref, xp_ref, xn_ref, cg_ref, win_ref, bin_ref, dww_ref, dwb_ref,
                     lng_ref, lnb_ref, wout_ref, bout_ref, fg_ref, wg_ref, wu_ref, wd_ref,
                     fin_ref, y_ref, u_ref, c_ref, x3_ref, act_ref, *, rc, tiles_per_seq):
    ts, D = xm_ref.shape[1], xm_ref.shape[2]
    i = pl.program_id(0)
    tile = jnp.minimum(i, pl.num_programs(0) - 2)
    pos = tile % tiles_per_seq

    @pl.when(i == 0)
    def _():
        x3_ref[1] = jnp.zeros((ts, D), F32)

    xm = xm_ref[0]
    xe = jnp.concatenate([xp_ref[0], xm, xn_ref[0]], axis=0)
    h = _rms(xe, cg_ref[...]).astype(BF16)
    u = jnp.dot(h, win_ref[...], preferred_element_type=F32) + bin_ref[...]
    u = u[:, :D] * jax.nn.sigmoid(u[:, D:])
    row = lax.broadcasted_iota(jnp.int32, (ts + 2 * HALO, 1), 0)
    valid = jnp.logical_and(jnp.logical_or(row >= HALO, pos > 0),
                            jnp.logical_or(row < ts + HALO, pos < tiles_per_seq - 1))
    u_ref[...] = jnp.where(valid, u, 0.0)

    x4 = _ffn(x3_ref[(i + 1) % 2], fg_ref, wg_ref, wu_ref, wd_ref, act_ref)
    y_ref[0] = _rms(x4, fin_ref[...])

    off = HALO - CONV_PAD
    win_rows = rc + 2 * HALO
    for r in range(ts // rc):
        for c in range(D // LANES):
            cols = slice(c * LANES, (c + 1) * LANES)
            window = u_ref[r * rc:r * rc + win_rows, cols]
            acc = jnp.broadcast_to(dwb_ref[:, cols], (rc, LANES))
            for phase in range(SUBLANES):
                shifted = window if phase == 0 else pltpu.roll(window, win_rows - phase, 0)
                for base in range(0, 2 * HALO, SUBLANES):
                    k = base + phase - off
                    if 0 <= k < CONV_WIDTH:
                        acc = acc + shifted[base:base + rc] * dww_ref[k:k + 1, cols]
            c_ref[r * rc:(r + 1) * rc, cols] = acc
    v = c_ref[...]
    mu = jnp.mean(v, axis=-1, keepdims=True)
    vc = v - mu
    var = jnp.mean(vc * vc, axis=-1, keepdims=True)
    v = vc * lax.rsqrt(var + LN_EPS) * lng_ref[...] + lnb_ref[...]
    v = (v * jax.nn.sigmoid(v)).astype(BF16)
    x3_ref[i % 2] = xm + jnp.dot(v, wout_ref[...], preferred_element_type=F32) + bout_ref[...]


def _conv_ffn_call(x, cg, win, bin_, dww, dwb, lng, lnb, wout, bout, fg, wg, wu, wd, fin, ts, rc):
    B, S, D = x.shape
    f = wg.shape[1]
    hb = ts // HALO
    n_hb = S // HALO
    n_seq = S // ts
    n_tiles = B * n_seq

    def conv_tile(i):
        t = jnp.minimum(i, n_tiles - 1)
        return t // n_seq, t % n_seq

    def main_map(i):
        b, pos = conv_tile(i)
        return b, pos, 0

    def prev_halo_map(i):
        b, pos = conv_tile(i)
        return b, jnp.maximum(pos * hb - 1, 0), 0

    def next_halo_map(i):
        b, pos = conv_tile(i)
        return b, jnp.minimum((pos + 1) * hb, n_hb - 1), 0

    def out_map(i):
        t = jnp.maximum(i - 1, 0)
        return t // n_seq, t % n_seq, 0

    return pl.pallas_call(
        functools.partial(_conv_ffn_kernel, rc=rc, tiles_per_seq=n_seq),
        grid=(n_tiles + 1,),
        in_specs=[
            pl.BlockSpec((1, ts, D), main_map),
            pl.BlockSpec((1, HALO, D), prev_halo_map),
            pl.BlockSpec((1, HALO, D), next_halo_map),
            _const_spec((1, D)), _const_spec((D, 2 * D)), _const_spec((1, 2 * D)),
            _const_spec(dww.shape), _const_spec((1, D)), _const_spec((1, D)), _const_spec((1, D)),
            _const_spec((D, D)), _const_spec((1, D)), _const_spec((1, D)),
            _const_spec((D, f)), _const_spec((D, f)), _const_spec((f, D)), _const_spec((1, D)),
        ],
        out_specs=pl.BlockSpec((1, ts, D), out_map),
        out_shape=jax.ShapeDtypeStruct((B, S, D), F32),
        scratch_shapes=[
            pltpu.VMEM((ts + 2 * HALO, D), F32),
            pltpu.VMEM((ts, D), F32),
            pltpu.VMEM((2, ts, D), F32),
            pltpu.VMEM((ts, f), BF16),
        ],
        compiler_params=pltpu.CompilerParams(
            dimension_semantics=("arbitrary",),
            vmem_limit_bytes=VMEM_LIMIT),
        name="conv_ffn_final",
    )(x, x, x, cg, win, bin_, dww, dwb, lng, lnb, wout, bout, fg, wg, wu, wd, fin)


def _rope_tables(seq_len):
    t = jnp.arange(seq_len, dtype=jnp.int32)
    row_ids = (t // GRID_W).astype(F32)
    col_ids = (t % GRID_W).astype(F32)
    inv_freq = ROPE_THETA ** (-jnp.arange(0, ROPE_AXIS_DIM, 2, dtype=F32) / ROPE_AXIS_DIM)
    ang_r = row_ids[:, None] * inv_freq[None, :]
    ang_c = col_ids[:, None] * inv_freq[None, :]
    zero = jnp.zeros_like(ang_r)
    cos = jnp.concatenate([jnp.cos(ang_r)] * 2 + [jnp.cos(ang_c)] * 2, axis=-1)
    sa = jnp.concatenate([zero, jnp.sin(ang_r), zero, jnp.sin(ang_c)], axis=-1)
    sb = jnp.concatenate([-jnp.sin(ang_r), zero, -jnp.sin(ang_c), zero], axis=-1)
    reps = LANES // HEAD_DIM
    return tuple(jnp.tile(a, (1, reps)) for a in (cos, sa, sb))


def _trunk(x, p, tables):
    B, S, D = x.shape
    cos, sa, sb = (t[:S] for t in tables)
    qT, k, vT = _qkv_call(x, p["attn_g"], p["w_qkv"], p["gqk"], p["seg"], cos, sa, sb,
                          tm=512, tq=256)
    o = lax.cond(p["score_bound"] <= SCORE_BOUND_MAX,
                 functools.partial(_flash_call, tq=256, tk=256, online=False),
                 functools.partial(_flash_call, tq=256, tk=256, online=True),
                 qT, k, vT)
    x2 = _attn_out_ffn_call(x.reshape(B * S, D), o.reshape(B * S, D), p["w_o"], p["ffn_g0"],
                            p["wg0"], p["wu0"], p["wd0"], tm=512)
    return _conv_ffn_call(x2.reshape(B, S, D), p["conv_g"], p["w_in"], p["b_in"], p["dw_w"],
                          p["dw_b"], p["ln_g"], p["ln_b"], p["w_out"], p["b_out"], p["ffn_g1"],
                          p["wg1"], p["wu1"], p["wd1"], p["fin_g"], ts=512, rc=64)


def kernel(x_prompt, x_sample, attn_norm_g, w_qkv, q_norm_g, k_norm_g, w_o, conv_norm_g, conv_w_in, conv_b_in, dw_w, dw_b, conv_ln_g, conv_ln_b, conv_w_out, conv_b_out, ffn_norm_g, w_gate, w_up, w_down, final_norm_g):
    row = lambda v: v.reshape(1, -1).astype(F32)
    scale = math.log2(math.e) / math.sqrt(HEAD_DIM)
    score_bound = (BF16_ROUNDING_MARGIN * HEAD_DIM * scale
                   * jnp.max(jnp.abs(q_norm_g[0])) * jnp.max(jnp.abs(k_norm_g[0])))
    gqk = jnp.concatenate([jnp.tile(q_norm_g[0] * scale, N_HEADS),
                           jnp.tile(k_norm_g[0], N_KV_HEADS)]).reshape(1, -1)
    lane = jnp.arange(MXU_N) // HEAD_DIM
    seg = jnp.where(lane[:, None] == lane[None, :], 1.0 / HEAD_DIM, 0.0).astype(BF16)
    p = dict(
        attn_g=row(attn_norm_g[0]), w_qkv=w_qkv[0].astype(BF16), gqk=gqk, seg=seg,
        w_o=w_o[0].astype(BF16), ffn_g0=row(ffn_norm_g[0]),
        wg0=w_gate[0].astype(BF16), wu0=w_up[0].astype(BF16), wd0=w_down[0].astype(BF16),
        conv_g=row(conv_norm_g[0]), w_in=conv_w_in[0].astype(BF16), b_in=row(conv_b_in[0]),
        dw_w=jnp.pad(dw_w[0], ((0, 1), (0, 0))), dw_b=row(dw_b[0]),
        ln_g=row(conv_ln_g[0]), ln_b=row(conv_ln_b[0]),
        w_out=conv_w_out[0].astype(BF16), b_out=row(conv_b_out[0]), ffn_g1=row(ffn_norm_g[1]),
        wg1=w_gate[1].astype(BF16), wu1=w_up[1].astype(BF16), wd1=w_down[1].astype(BF16),
        fin_g=row(final_norm_g), score_bound=score_bound,
    )
    tables = _rope_tables(max(x_prompt.shape[1], x_sample.shape[1]))
    return (_trunk(x_prompt, p, tables), _trunk(x_sample, p, tables))
```

```python
import functools
import math

import jax
import jax.numpy as jnp
from jax import lax
from jax.experimental import pallas as pl
from jax.experimental.pallas import tpu as pltpu

HEAD_DIM = 64
N_HEADS = 16
N_KV_HEADS = 4
GROUP = N_HEADS // N_KV_HEADS
GRID_W = 64
ROPE_AXIS_DIM = HEAD_DIM // 2
ROPE_HALF = ROPE_AXIS_DIM // 2
ROPE_THETA = 10000.0
CONV_WIDTH = 31
CONV_PAD = CONV_WIDTH // 2
RMS_EPS = 1e-6
LN_EPS = 1e-5

LANES = 128
SUBLANES = 8
MXU_N = 256
HALO = 16
BF16_SUBLANES = 16
V_ROWS = HEAD_DIM + BF16_SUBLANES
VMEM_LIMIT = 56 * 1024 * 1024

F32 = jnp.float32
BF16 = jnp.bfloat16


def _const_spec(shape):
    zeros = (0,) * len(shape)
    return pl.BlockSpec(shape, lambda *_: zeros, pipeline_mode=pl.Buffered(1))


def _rms(x, g):
    ms = jnp.mean(x * x, axis=-1, keepdims=True)
    return x * lax.rsqrt(ms + RMS_EPS) * g


def _qkv_kernel(x_ref, g_ref, w_ref, gqk_ref, seg_ref, cos_ref, sa_ref, sb_ref,
                qT_ref, k_ref, vT_ref, *, tq):
    tm = x_ref.shape[1]
    q_dim = N_HEADS * HEAD_DIM
    qk_dim = q_dim + N_KV_HEADS * HEAD_DIM
    h = _rms(x_ref[0], g_ref[...]).astype(BF16)

    def project(c):
        return jnp.dot(h, w_ref[:, c * MXU_N:(c + 1) * MXU_N], preferred_element_type=F32)

    seg = seg_ref[...]
    cos, sa, sb = cos_ref[...], sa_ref[...], sb_ref[...]
    z_next = project(0)
    for c in range(qk_dim // MXU_N):
        z, z_next = z_next, project(c + 1)
        y = z * z
        y_hi = y.astype(BF16)
        y_lo = (y - y_hi.astype(F32)).astype(BF16)
        ms = (jnp.dot(y_hi, seg, preferred_element_type=F32)
              + jnp.dot(y_lo, seg, preferred_element_type=F32))
        zn = z * lax.rsqrt(ms + RMS_EPS) * gqk_ref[:, c * MXU_N:(c + 1) * MXU_N]
        for half in range(MXU_N // LANES):
            zz = zn[:, half * LANES:(half + 1) * LANES]
            r = (zz * cos
                 + pltpu.roll(zz, ROPE_HALF, 1) * sa
                 + pltpu.roll(zz, LANES - ROPE_HALF, 1) * sb)
            col = c * MXU_N + half * LANES
            if col < q_dim:
                rT = r.T.astype(BF16)
                for hh in range(LANES // HEAD_DIM):
                    g = half * (LANES // HEAD_DIM) + hh
                    for tt in range(tm // tq):
                        dst = (tt * GROUP + g) * tq
                        qT_ref[0, c, :, dst:dst + tq] = rT[hh * HEAD_DIM:(hh + 1) * HEAD_DIM,
                                                           tt * tq:(tt + 1) * tq]
            else:
                kv = (col - q_dim) // HEAD_DIM
                k_ref[0, kv] = r[:, :HEAD_DIM].astype(BF16)
                k_ref[0, kv + 1] = r[:, HEAD_DIM:].astype(BF16)
    assert w_ref.shape[1] - qk_dim == MXU_N
    vT = z_next.T.astype(BF16)
    ones_rows = (lax.broadcasted_iota(jnp.int32, (V_ROWS - HEAD_DIM, tm), 0) == 0).astype(BF16)
    for kv in range(N_KV_HEADS):
        vT_ref[0, kv, :HEAD_DIM, :] = vT[kv * HEAD_DIM:(kv + 1) * HEAD_DIM]
        vT_ref[0, kv, HEAD_DIM:, :] = ones_rows


def _qkv_call(x, g, w_qkv, gqk, seg, cos, sa, sb, tm, tq):
    B, S, D = x.shape
    n_s = S // tm
    qkv_dim = w_qkv.shape[1]
    kv_dim = N_KV_HEADS * HEAD_DIM
    tab_spec = pl.BlockSpec((tm, LANES), lambda b, i: (i, 0))
    return pl.pallas_call(
        functools.partial(_qkv_kernel, tq=tq),
        grid=(B, n_s),
        in_specs=[
            pl.BlockSpec((1, tm, D), lambda b, i: (b, i, 0)),
            _const_spec((1, D)),
            _const_spec((D, qkv_dim)),
            _const_spec((1, gqk.shape[1])),
            _const_spec((MXU_N, MXU_N)),
            tab_spec, tab_spec, tab_spec,
        ],
        out_specs=[
            pl.BlockSpec((1, N_KV_HEADS, HEAD_DIM, GROUP * tm), lambda b, i: (b, 0, 0, i)),
            pl.BlockSpec((1, N_KV_HEADS, tm, HEAD_DIM), lambda b, i: (b, 0, i, 0)),
            pl.BlockSpec((1, N_KV_HEADS, V_ROWS, tm), lambda b, i: (b, 0, 0, i)),
        ],
        out_shape=[
            jax.ShapeDtypeStruct((B, N_KV_HEADS, HEAD_DIM, GROUP * S), BF16),
            jax.ShapeDtypeStruct((B, N_KV_HEADS, S, HEAD_DIM), BF16),
            jax.ShapeDtypeStruct((B, N_KV_HEADS, V_ROWS, S), BF16),
        ],
        compiler_params=pltpu.CompilerParams(
            dimension_semantics=("parallel", "parallel"),
            vmem_limit_bytes=VMEM_LIMIT),
        name="qkv_rope",
    )(x, g, w_qkv, gqk, seg, cos, sa, sb)


NEG_BIG = -1e30
SCORE_BOUND_MAX = 60.0
BF16_ROUNDING_MARGIN = 1.02


S_SLOTS = 4
P_SLOTS = 2
LOOKAHEAD = 2
UNROLL = 8
FLASH_TQ = 512
FLASH_TK = 256


def _flash_kernel(qT_ref, k_ref, vT_ref, o_ref, s_ref, p_ref, mx_ref, m_ref, acc_ref,
                  *, tk, tq):
    S = k_ref.shape[2]
    n = S // tk

    def scores(c, c_static):
        start = pl.multiple_of(c * tk, tk)
        s = jnp.dot(k_ref[0, 0, pl.ds(start, tk), :], qT_ref[0, 0],
                    preferred_element_type=F32)
        s_ref[c_static % S_SLOTS] = s
        mx_ref[c_static % S_SLOTS] = jnp.max(s, axis=0, keepdims=True)

    def pv(c, c_static):
        start = pl.multiple_of(c * tk, tk)
        return jnp.dot(vT_ref[0, 0, :, pl.ds(start, tk)], p_ref[c_static % P_SLOTS],
                       preferred_element_type=F32)

    def step(base, u, prefetch):
        i = base + u
        if prefetch:
            scores(i + LOOKAHEAD, u + LOOKAHEAD)
        pv_prev = pv(jnp.maximum(i - 1, 0), u - 1)
        m_old = m_ref[...]
        m_new = jnp.maximum(m_old, mx_ref[u % S_SLOTS])
        alpha = jnp.exp2(m_old - m_new)
        p_ref[u % P_SLOTS] = jnp.exp2(s_ref[u % S_SLOTS] - m_new).astype(BF16)
        m_ref[...] = m_new
        acc_ref[...] = (acc_ref[...] + pv_prev) * alpha

    m_ref[...] = jnp.full(m_ref.shape, NEG_BIG, F32)
    acc_ref[...] = jnp.zeros(acc_ref.shape, F32)
    p_ref[(-1) % P_SLOTS] = jnp.zeros(p_ref.shape[1:], BF16)
    for c in range(LOOKAHEAD):
        scores(c, c)

    def group(j, carry):
        for u in range(UNROLL):
            step(j * UNROLL, u, True)
        return carry

    lax.fori_loop(0, n // UNROLL - 1, group, 0)
    for u in range(UNROLL):
        step(n - UNROLL, u, u + LOOKAHEAD < UNROLL)
    _flash_finish(acc_ref[...] + pv(n - 1, n - 1), o_ref, tq)


def _flash_finish(acc, o_ref, tq):
    o = acc[:HEAD_DIM] * (1.0 / acc[HEAD_DIM:HEAD_DIM + 1])
    o_ref[0] = jnp.concatenate(
        [o[:, g * tq:(g + 1) * tq].T for g in range(GROUP)], axis=1).astype(BF16)


def _flash_bounded_kernel(qT_ref, k_ref, vT_ref, o_ref, p_ref, acc_ref, *, tk, tq):
    S = k_ref.shape[2]
    n = S // tk

    def numerators(c, c_static):
        start = pl.multiple_of(c * tk, tk)
        s = jnp.dot(k_ref[0, 0, pl.ds(start, tk), :], qT_ref[0, 0],
                    preferred_element_type=F32)
        p_ref[c_static % S_SLOTS] = jnp.exp2(s).astype(BF16)

    def step(base, u, prefetch):
        i = base + u
        if prefetch:
            numerators(i + LOOKAHEAD, u + LOOKAHEAD)
        start = pl.multiple_of(i * tk, tk)
        acc_ref[...] += jnp.dot(vT_ref[0, 0, :, pl.ds(start, tk)], p_ref[u % S_SLOTS],
                                preferred_element_type=F32)

    acc_ref[...] = jnp.zeros(acc_ref.shape, F32)
    for c in range(LOOKAHEAD):
        numerators(c, c)

    def group(j, carry):
        for u in range(UNROLL):
            step(j * UNROLL, u, True)
        return carry

    lax.fori_loop(0, n // UNROLL - 1, group, 0)
    for u in range(UNROLL):
        step(n - UNROLL, u, u + LOOKAHEAD < UNROLL)
    _flash_finish(acc_ref[...], o_ref, tq)


def _flash_call(qT, k, vT, tq, tk, online):
    B, _, S, _ = k.shape
    n_cols = GROUP * tq
    assert S % (UNROLL * tk) == 0 and UNROLL % S_SLOTS == 0 and UNROLL % P_SLOTS == 0
    assert LOOKAHEAD < S_SLOTS
    acc_scratch = pltpu.VMEM((V_ROWS, n_cols), F32)
    if online:
        body = _flash_kernel
        scratch = [
            pltpu.VMEM((S_SLOTS, tk, n_cols), F32),
            pltpu.VMEM((P_SLOTS, tk, n_cols), BF16),
            pltpu.VMEM((S_SLOTS, 1, n_cols), F32),
            pltpu.VMEM((1, n_cols), F32),
            acc_scratch,
        ]
    else:
        body = _flash_bounded_kernel
        scratch = [pltpu.VMEM((S_SLOTS, tk, n_cols), BF16), acc_scratch]
    return pl.pallas_call(
        functools.partial(body, tk=tk, tq=tq),
        grid=(B, N_KV_HEADS, S // tq),
        in_specs=[
            pl.BlockSpec((1, 1, HEAD_DIM, n_cols), lambda b, j, i: (b, j, 0, i)),
            pl.BlockSpec((1, 1, S, HEAD_DIM), lambda b, j, i: (b, j, 0, 0)),
            pl.BlockSpec((1, 1, V_ROWS, S), lambda b, j, i: (b, j, 0, 0)),
        ],
        out_specs=pl.BlockSpec((1, tq, GROUP * HEAD_DIM), lambda b, j, i: (b, i, j)),
        out_shape=jax.ShapeDtypeStruct((B, S, N_HEADS * HEAD_DIM), BF16),
        scratch_shapes=scratch,
        compiler_params=pltpu.CompilerParams(
            dimension_semantics=("parallel", "parallel", "parallel"),
            vmem_limit_bytes=VMEM_LIMIT),
        name="flash_online" if online else "flash_bounded",
    )(qT, k, vT)


def _ffn(x, g_ref, wg_ref, wu_ref, wd_ref, act_ref):
    h = _rms(x, g_ref[...]).astype(BF16)
    f = wg_ref.shape[1]
    for c in range(f // MXU_N):
        cols = slice(c * MXU_N, (c + 1) * MXU_N)
        gate = jnp.dot(h, wg_ref[:, cols], preferred_element_type=F32)
        up = jnp.dot(h, wu_ref[:, cols], preferred_element_type=F32)
        act_ref[:, cols] = (gate * jax.nn.sigmoid(gate) * up).astype(BF16)
    return x + jnp.dot(act_ref[...], wd_ref[...], preferred_element_type=F32)


def _attn_out_ffn_kernel(x_ref, o_ref, wo_ref, g_ref, wg_ref, wu_ref, wd_ref,
                         y_ref, act_ref):
    x1 = x_ref[...] + jnp.dot(o_ref[...], wo_ref[...], preferred_element_type=F32)
    y_ref[...] = _ffn(x1, g_ref, wg_ref, wu_ref, wd_ref, act_ref)


def _attn_out_ffn_call(x, o, wo, g, wg, wu, wd, tm):
    T, D = x.shape
    f = wg.shape[1]
    row_spec = pl.BlockSpec((tm, D), lambda i: (i, 0))
    return pl.pallas_call(
        _attn_out_ffn_kernel,
        grid=(T // tm,),
        in_specs=[row_spec, row_spec, _const_spec((D, D)), _const_spec((1, D)),
                  _const_spec((D, f)), _const_spec((D, f)), _const_spec((f, D))],
        out_specs=row_spec,
        out_shape=jax.ShapeDtypeStruct((T, D), F32),
        scratch_shapes=[pltpu.VMEM((tm, f), BF16)],
        compiler_params=pltpu.CompilerParams(
            dimension_semantics=("parallel",),
            vmem_limit_bytes=VMEM_LIMIT),
        name="attn_out_ffn",
    )(x, o, wo, g, wg, wu, wd)


def _conv_ffn_kernel(xm_ref, xp_ref, xn_ref, cg_ref, win_ref, bin_ref, dww_ref, dwb_ref,
                     lng_ref, lnb_ref, wout_ref, bout_ref, fg_ref, wg_ref, wu_ref, wd_ref,
                     fin_ref, y_ref, u_ref, c_ref, x3_ref, act_ref, *, rc, tiles_per_seq):
    ts, D = xm_ref.shape[1], xm_ref.shape[2]
    i = pl.program_id(0)
    tile = jnp.minimum(i, pl.num_programs(0) - 2)
    pos = tile % tiles_per_seq

    @pl.when(i == 0)
    def _():
        x3_ref[1] = jnp.zeros((ts, D), F32)

    xm = xm_ref[0]
    xe = jnp.concatenate([xp_ref[0], xm, xn_ref[0]], axis=0)
    h = _rms(xe, cg_ref[...]).astype(BF16)
    u = jnp.dot(h, win_ref[...], preferred_element_type=F32) + bin_ref[...]
    u = u[:, :D] * jax.nn.sigmoid(u[:, D:])
    row = lax.broadcasted_iota(jnp.int32, (ts + 2 * HALO, 1), 0)
    valid = jnp.logical_and(jnp.logical_or(row >= HALO, pos > 0),
                            jnp.logical_or(row < ts + HALO, pos < tiles_per_seq - 1))
    u_ref[...] = jnp.where(valid, u, 0.0)

    x4 = _ffn(x3_ref[(i + 1) % 2], fg_ref, wg_ref, wu_ref, wd_ref, act_ref)
    y_ref[0] = _rms(x4, fin_ref[...])

    off = HALO - CONV_PAD
    win_rows = rc + 2 * HALO
    for r in range(ts // rc):
        for c in range(D // LANES):
            cols = slice(c * LANES, (c + 1) * LANES)
            window = u_ref[r * rc:r * rc + win_rows, cols]
            acc = jnp.broadcast_to(dwb_ref[:, cols], (rc, LANES))
            for phase in range(SUBLANES):
                shifted = window if phase == 0 else pltpu.roll(window, win_rows - phase, 0)
                for base in range(0, 2 * HALO, SUBLANES):
                    k = base + phase - off
                    if 0 <= k < CONV_WIDTH:
                        acc = acc + shifted[base:base + rc] * dww_ref[k:k + 1, cols]
            c_ref[r * rc:(r + 1) * rc, cols] = acc
    v = c_ref[...]
    mu = jnp.mean(v, axis=-1, keepdims=True)
    vc = v - mu
    var = jnp.mean(vc * vc, axis=-1, keepdims=True)
    v = vc * lax.rsqrt(var + LN_EPS) * lng_ref[...] + lnb_ref[...]
    v = (v * jax.nn.sigmoid(v)).astype(BF16)
    x3_ref[i % 2] = xm + jnp.dot(v, wout_ref[...], preferred_element_type=F32) + bout_ref[...]


def _conv_ffn_call(x, cg, win, bin_, dww, dwb, lng, lnb, wout, bout, fg, wg, wu, wd, fin, ts, rc):
    B, S, D = x.shape
    f = wg.shape[1]
    hb = ts // HALO
    n_hb = S // HALO
    n_seq = S // ts
    n_tiles = B * n_seq

    def conv_tile(i):
        t = jnp.minimum(i, n_tiles - 1)
        return t // n_seq, t % n_seq

    def main_map(i):
        b, pos = conv_tile(i)
        return b, pos, 0

    def prev_halo_map(i):
        b, pos = conv_tile(i)
        return b, jnp.maximum(pos * hb - 1, 0), 0

    def next_halo_map(i):
        b, pos = conv_tile(i)
        return b, jnp.minimum((pos + 1) * hb, n_hb - 1), 0

    def out_map(i):
        t = jnp.maximum(i - 1, 0)
        return t // n_seq, t % n_seq, 0

    return pl.pallas_call(
        functools.partial(_conv_ffn_kernel, rc=rc, tiles_per_seq=n_seq),
        grid=(n_tiles + 1,),
        in_specs=[
            pl.BlockSpec((1, ts, D), main_map),
            pl.BlockSpec((1, HALO, D), prev_halo_map),
            pl.BlockSpec((1, HALO, D), next_halo_map),
            _const_spec((1, D)), _const_spec((D, 2 * D)), _const_spec((1, 2 * D)),
            _const_spec(dww.shape), _const_spec((1, D)), _const_spec((1, D)), _const_spec((1, D)),
            _const_spec((D, D)), _const_spec((1, D)), _const_spec((1, D)),
            _const_spec((D, f)), _const_spec((D, f)), _const_spec((f, D)), _const_spec((1, D)),
        ],
        out_specs=pl.BlockSpec((1, ts, D), out_map),
        out_shape=jax.ShapeDtypeStruct((B, S, D), F32),
        scratch_shapes=[
            pltpu.VMEM((ts + 2 * HALO, D), F32),
            pltpu.VMEM((ts, D), F32),
            pltpu.VMEM((2, ts, D), F32),
            pltpu.VMEM((ts, f), BF16),
        ],
        compiler_params=pltpu.CompilerParams(
            dimension_semantics=("arbitrary",),
            vmem_limit_bytes=VMEM_LIMIT),
        name="conv_ffn_final",
    )(x, x, x, cg, win, bin_, dww, dwb, lng, lnb, wout, bout, fg, wg, wu, wd, fin)


def _rope_tables(seq_len):
    t = jnp.arange(seq_len, dtype=jnp.int32)
    row_ids = (t // GRID_W).astype(F32)
    col_ids = (t % GRID_W).astype(F32)
    inv_freq = ROPE_THETA ** (-jnp.arange(0, ROPE_AXIS_DIM, 2, dtype=F32) / ROPE_AXIS_DIM)
    ang_r = row_ids[:, None] * inv_freq[None, :]
    ang_c = col_ids[:, None] * inv_freq[None, :]
    zero = jnp.zeros_like(ang_r)
    cos = jnp.concatenate([jnp.cos(ang_r)] * 2 + [jnp.cos(ang_c)] * 2, axis=-1)
    sa = jnp.concatenate([zero, jnp.sin(ang_r), zero, jnp.sin(ang_c)], axis=-1)
    sb = jnp.concatenate([-jnp.sin(ang_r), zero, -jnp.sin(ang_c), zero], axis=-1)
    reps = LANES // HEAD_DIM
    return tuple(jnp.tile(a, (1, reps)) for a in (cos, sa, sb))


def _trunk(x, p, tables):
    B, S, D = x.shape
    cos, sa, sb = (t[:S] for t in tables)
    qT, k, vT = _qkv_call(x, p["attn_g"], p["w_qkv"], p["gqk"], p["seg"], cos, sa, sb,
                          tm=512, tq=FLASH_TQ)
    o = lax.cond(p["score_bound"] <= SCORE_BOUND_MAX,
                 functools.partial(_flash_call, tq=FLASH_TQ, tk=FLASH_TK, online=False),
                 functools.partial(_flash_call, tq=FLASH_TQ, tk=FLASH_TK, online=True),
                 qT, k, vT)
    x2 = _attn_out_ffn_call(x.reshape(B * S, D), o.reshape(B * S, D), p["w_o"], p["ffn_g0"],
                            p["wg0"], p["wu0"], p["wd0"], tm=512)
    return _conv_ffn_call(x2.reshape(B, S, D), p["conv_g"], p["w_in"], p["b_in"], p["dw_w"],
                          p["dw_b"], p["ln_g"], p["ln_b"], p["w_out"], p["b_out"], p["ffn_g1"],
                          p["wg1"], p["wu1"], p["wd1"], p["fin_g"], ts=512, rc=64)


def kernel(x_prompt, x_sample, attn_norm_g, w_qkv, q_norm_g, k_norm_g, w_o, conv_norm_g, conv_w_in, conv_b_in, dw_w, dw_b, conv_ln_g, conv_ln_b, conv_w_out, conv_b_out, ffn_norm_g, w_gate, w_up, w_down, final_norm_g):
    row = lambda v: v.reshape(1, -1).astype(F32)
    scale = math.log2(math.e) / math.sqrt(HEAD_DIM)
    score_bound = (BF16_ROUNDING_MARGIN * HEAD_DIM * scale
                   * jnp.max(jnp.abs(q_norm_g[0])) * jnp.max(jnp.abs(k_norm_g[0])))
    gqk = jnp.concatenate([jnp.tile(q_norm_g[0] * scale, N_HEADS),
                           jnp.tile(k_norm_g[0], N_KV_HEADS)]).reshape(1, -1)
    lane = jnp.arange(MXU_N) // HEAD_DIM
    seg = jnp.where(lane[:, None] == lane[None, :], 1.0 / HEAD_DIM, 0.0).astype(BF16)
    p = dict(
        attn_g=row(attn_norm_g[0]), w_qkv=w_qkv[0].astype(BF16), gqk=gqk, seg=seg,
        w_o=w_o[0].astype(BF16), ffn_g0=row(ffn_norm_g[0]),
        wg0=w_gate[0].astype(BF16), wu0=w_up[0].astype(BF16), wd0=w_down[0].astype(BF16),
        conv_g=row(conv_norm_g[0]), w_in=conv_w_in[0].astype(BF16), b_in=row(conv_b_in[0]),
        dw_w=jnp.pad(dw_w[0], ((0, 1), (0, 0))), dw_b=row(dw_b[0]),
        ln_g=row(conv_ln_g[0]), ln_b=row(conv_ln_b[0]),
        w_out=conv_w_out[0].astype(BF16), b_out=row(conv_b_out[0]), ffn_g1=row(ffn_norm_g[1]),
        wg1=w_gate[1].astype(BF16), wu1=w_up[1].astype(BF16), wd1=w_down[1].astype(BF16),
        fin_g=row(final_norm_g), score_bound=score_bound,
    )
    tables = _rope_tables(max(x_prompt.shape[1], x_sample.shape[1]))
    return (_trunk(x_prompt, p, tables), _trunk(x_sample, p, tables))
```

```python
import functools
import math

import jax
import jax.numpy as jnp
from jax import lax
from jax.experimental import pallas as pl
from jax.experimental.pallas import tpu as pltpu

HEAD_DIM = 64
N_HEADS = 16
N_KV_HEADS = 4
GROUP = N_HEADS // N_KV_HEADS
GRID_W = 64
ROPE_AXIS_DIM = HEAD_DIM // 2
ROPE_HALF = ROPE_AXIS_DIM // 2
ROPE_THETA = 10000.0
CONV_WIDTH = 31
CONV_PAD = CONV_WIDTH // 2
RMS_EPS = 1e-6
LN_EPS = 1e-5

LANES = 128
SUBLANES = 8
MXU_N = 256
HALO = 16
BF16_SUBLANES = 16
V_ROWS = HEAD_DIM + BF16_SUBLANES
VMEM_LIMIT = 56 * 1024 * 1024

F32 = jnp.float32
BF16 = jnp.bfloat16


def _const_spec(shape):
    zeros = (0,) * len(shape)
    return pl.BlockSpec(shape, lambda *_: zeros, pipeline_mode=pl.Buffered(1))


def _rms(x, g):
    ms = jnp.mean(x * x, axis=-1, keepdims=True)
    return x * lax.rsqrt(ms + RMS_EPS) * g


def _qkv_kernel(x_ref, g_ref, w_ref, gqk_ref, seg_ref, cos_ref, sa_ref, sb_ref,
                qT_ref, k_ref, vT_ref, *, tq):
    tm = x_ref.shape[1]
    q_dim = N_HEADS * HEAD_DIM
    qk_dim = q_dim + N_KV_HEADS * HEAD_DIM
    h = _rms(x_ref[0], g_ref[...]).astype(BF16)

    def project(c):
        return jnp.dot(h, w_ref[:, c * MXU_N:(c + 1) * MXU_N], preferred_element_type=F32)

    seg = seg_ref[...]
    cos, sa, sb = cos_ref[...], sa_ref[...], sb_ref[...]
    z_next = project(0)
    for c in range(qk_dim // MXU_N):
        z, z_next = z_next, project(c + 1)
        y = z * z
        y_hi = y.astype(BF16)
        y_lo = (y - y_hi.astype(F32)).astype(BF16)
        ms = (jnp.dot(y_hi, seg, preferred_element_type=F32)
              + jnp.dot(y_lo, seg, preferred_element_type=F32))
        zn = z * lax.rsqrt(ms + RMS_EPS) * gqk_ref[:, c * MXU_N:(c + 1) * MXU_N]
        for half in range(MXU_N // LANES):
            zz = zn[:, half * LANES:(half + 1) * LANES]
            r = (zz * cos
                 + pltpu.roll(zz, ROPE_HALF, 1) * sa
                 + pltpu.roll(zz, LANES - ROPE_HALF, 1) * sb)
            col = c * MXU_N + half * LANES
            if col < q_dim:
                rT = r.T.astype(BF16)
                for hh in range(LANES // HEAD_DIM):
                    g = half * (LANES // HEAD_DIM) + hh
                    for tt in range(tm // tq):
                        dst = (tt * GROUP + g) * tq
                        qT_ref[0, c, :, dst:dst + tq] = rT[hh * HEAD_DIM:(hh + 1) * HEAD_DIM,
                                                           tt * tq:(tt + 1) * tq]
            else:
                kv = (col - q_dim) // HEAD_DIM
                k_ref[0, kv] = r[:, :HEAD_DIM].astype(BF16)
                k_ref[0, kv + 1] = r[:, HEAD_DIM:].astype(BF16)
    assert w_ref.shape[1] - qk_dim == MXU_N
    vT = z_next.T.astype(BF16)
    ones_rows = (lax.broadcasted_iota(jnp.int32, (V_ROWS - HEAD_DIM, tm), 0) == 0).astype(BF16)
    for kv in range(N_KV_HEADS):
        vT_ref[0, kv, :HEAD_DIM, :] = vT[kv * HEAD_DIM:(kv + 1) * HEAD_DIM]
        vT_ref[0, kv, HEAD_DIM:, :] = ones_rows


def _qkv_call(x, g, w_qkv, gqk, seg, cos, sa, sb, tm, tq):
    B, S, D = x.shape
    n_s = S // tm
    qkv_dim = w_qkv.shape[1]
    tab_spec = pl.BlockSpec((tm, LANES), lambda b, i: (i, 0))
    return pl.pallas_call(
        functools.partial(_qkv_kernel, tq=tq),
        grid=(B, n_s),
        in_specs=[
            pl.BlockSpec((1, tm, D), lambda b, i: (b, i, 0)),
            _const_spec((1, D)),
            _const_spec((D, qkv_dim)),
            _const_spec((1, gqk.shape[1])),
            _const_spec((MXU_N, MXU_N)),
            tab_spec, tab_spec, tab_spec,
        ],
        out_specs=[
            pl.BlockSpec((1, N_KV_HEADS, HEAD_DIM, GROUP * tm), lambda b, i: (b, 0, 0, i)),
            pl.BlockSpec((1, N_KV_HEADS, tm, HEAD_DIM), lambda b, i: (b, 0, i, 0)),
            pl.BlockSpec((1, N_KV_HEADS, V_ROWS, tm), lambda b, i: (b, 0, 0, i)),
        ],
        out_shape=[
            jax.ShapeDtypeStruct((B, N_KV_HEADS, HEAD_DIM, GROUP * S), BF16),
            jax.ShapeDtypeStruct((B, N_KV_HEADS, S, HEAD_DIM), BF16),
            jax.ShapeDtypeStruct((B, N_KV_HEADS, V_ROWS, S), BF16),
        ],
        compiler_params=pltpu.CompilerParams(
            dimension_semantics=("parallel", "parallel"),
            vmem_limit_bytes=VMEM_LIMIT),
        name="qkv_rope",
    )(x, g, w_qkv, gqk, seg, cos, sa, sb)


NEG_BIG = -1e30
SCORE_BOUND_MAX = 60.0
BF16_ROUNDING_MARGIN = 1.02


S_SLOTS = 4
P_SLOTS = 2
LOOKAHEAD = 2
UNROLL = 8
FLASH_TQ = 512
FLASH_TK = 256


def _flash_kernel(qT_ref, k_ref, vT_ref, o_ref, s_ref, p_ref, mx_ref, m_ref, acc_ref,
                  *, tk, tq):
    S = k_ref.shape[2]
    n = S // tk

    def scores(c, c_static):
        start = pl.multiple_of(c * tk, tk)
        s = jnp.dot(k_ref[0, 0, pl.ds(start, tk), :], qT_ref[0, 0],
                    preferred_element_type=F32)
        s_ref[c_static % S_SLOTS] = s
        mx_ref[c_static % S_SLOTS] = jnp.max(s, axis=0, keepdims=True)

    def pv(c, c_static):
        start = pl.multiple_of(c * tk, tk)
        return jnp.dot(vT_ref[0, 0, :, pl.ds(start, tk)], p_ref[c_static % P_SLOTS],
                       preferred_element_type=F32)

    def step(base, u, prefetch):
        i = base + u
        if prefetch:
            scores(i + LOOKAHEAD, u + LOOKAHEAD)
        pv_prev = pv(jnp.maximum(i - 1, 0), u - 1)
        m_old = m_ref[...]
        m_new = jnp.maximum(m_old, mx_ref[u % S_SLOTS])
        alpha = jnp.exp2(m_old - m_new)
        p_ref[u % P_SLOTS] = jnp.exp2(s_ref[u % S_SLOTS] - m_new).astype(BF16)
        m_ref[...] = m_new
        acc_ref[...] = (acc_ref[...] + pv_prev) * alpha

    m_ref[...] = jnp.full(m_ref.shape, NEG_BIG, F32)
    acc_ref[...] = jnp.zeros(acc_ref.shape, F32)
    p_ref[(-1) % P_SLOTS] = jnp.zeros(p_ref.shape[1:], BF16)
    for c in range(LOOKAHEAD):
        scores(c, c)

    def group(j, carry):
        for u in range(UNROLL):
            step(j * UNROLL, u, True)
        return carry

    lax.fori_loop(0, n // UNROLL - 1, group, 0)
    for u in range(UNROLL):
        step(n - UNROLL, u, u + LOOKAHEAD < UNROLL)
    acc = acc_ref[...] + pv(n - 1, n - 1)
    _flash_finish(acc[:HEAD_DIM], acc[HEAD_DIM:HEAD_DIM + 1], o_ref, tq)


def _flash_finish(o_t, denom, o_ref, tq):
    o = o_t * (1.0 / denom)
    o_ref[0] = jnp.concatenate(
        [o[:, g * tq:(g + 1) * tq].T for g in range(GROUP)], axis=1).astype(BF16)


def _flash_bounded_kernel(qT_ref, k_ref, vT_ref, o_ref, p_ref, l_ref, acc_ref, *, tk, tq):
    S = k_ref.shape[2]
    n = S // tk

    def numerators(c, c_static):
        start = pl.multiple_of(c * tk, tk)
        s = jnp.dot(k_ref[0, 0, pl.ds(start, tk), :], qT_ref[0, 0],
                    preferred_element_type=F32)
        p = jnp.exp2(s)
        l_ref[...] += sum(p[j * SUBLANES:(j + 1) * SUBLANES] for j in range(tk // SUBLANES))
        p_ref[c_static % S_SLOTS] = p.astype(BF16)

    def step(base, u, prefetch):
        i = base + u
        if prefetch:
            numerators(i + LOOKAHEAD, u + LOOKAHEAD)
        start = pl.multiple_of(i * tk, tk)
        acc_ref[...] += jnp.dot(vT_ref[0, 0, :HEAD_DIM, pl.ds(start, tk)], p_ref[u % S_SLOTS],
                                preferred_element_type=F32)

    acc_ref[...] = jnp.zeros(acc_ref.shape, F32)
    l_ref[...] = jnp.zeros(l_ref.shape, F32)
    for c in range(LOOKAHEAD):
        numerators(c, c)

    def group(j, carry):
        for u in range(UNROLL):
            step(j * UNROLL, u, True)
        return carry

    lax.fori_loop(0, n // UNROLL - 1, group, 0)
    for u in range(UNROLL):
        step(n - UNROLL, u, u + LOOKAHEAD < UNROLL)
    _flash_finish(acc_ref[...], jnp.sum(l_ref[...], axis=0, keepdims=True), o_ref, tq)


def _flash_call(qT, k, vT, tq, tk, online):
    B, _, S, _ = k.shape
    n_cols = GROUP * tq
    assert S % (UNROLL * tk) == 0 and UNROLL % S_SLOTS == 0 and UNROLL % P_SLOTS == 0
    assert LOOKAHEAD < S_SLOTS
    if online:
        body = _flash_kernel
        scratch = [
            pltpu.VMEM((S_SLOTS, tk, n_cols), F32),
            pltpu.VMEM((P_SLOTS, tk, n_cols), BF16),
            pltpu.VMEM((S_SLOTS, 1, n_cols), F32),
            pltpu.VMEM((1, n_cols), F32),
            pltpu.VMEM((V_ROWS, n_cols), F32),
        ]
    else:
        body = _flash_bounded_kernel
        scratch = [
            pltpu.VMEM((S_SLOTS, tk, n_cols), BF16),
            pltpu.VMEM((SUBLANES, n_cols), F32),
            pltpu.VMEM((HEAD_DIM, n_cols), F32),
        ]
    return pl.pallas_call(
        functools.partial(body, tk=tk, tq=tq),
        grid=(B, N_KV_HEADS, S // tq),
        in_specs=[
            pl.BlockSpec((1, 1, HEAD_DIM, n_cols), lambda b, j, i: (b, j, 0, i)),
            pl.BlockSpec((1, 1, S, HEAD_DIM), lambda b, j, i: (b, j, 0, 0)),
            pl.BlockSpec((1, 1, V_ROWS, S), lambda b, j, i: (b, j, 0, 0)),
        ],
        out_specs=pl.BlockSpec((1, tq, GROUP * HEAD_DIM), lambda b, j, i: (b, i, j)),
        out_shape=jax.ShapeDtypeStruct((B, S, N_HEADS * HEAD_DIM), BF16),
        scratch_shapes=scratch,
        compiler_params=pltpu.CompilerParams(
            dimension_semantics=("parallel", "parallel", "parallel"),
            vmem_limit_bytes=VMEM_LIMIT),
        name="flash_online" if online else "flash_bounded",
    )(qT, k, vT)


def _ffn(x, g_ref, wg_ref, wu_ref, wd_ref, act_ref):
    h = _rms(x, g_ref[...]).astype(BF16)
    f = wg_ref.shape[1]
    for c in range(f // MXU_N):
        cols = slice(c * MXU_N, (c + 1) * MXU_N)
        gate = jnp.dot(h, wg_ref[:, cols], preferred_element_type=F32)
        up = jnp.dot(h, wu_ref[:, cols], preferred_element_type=F32)
        act_ref[:, cols] = (gate * jax.nn.sigmoid(gate) * up).astype(BF16)
    return x + jnp.dot(act_ref[...], wd_ref[...], preferred_element_type=F32)


def _attn_out_ffn_kernel(x_ref, o_ref, wo_ref, g_ref, wg_ref, wu_ref, wd_ref,
                         y_ref, act_ref):
    x1 = x_ref[...] + jnp.dot(o_ref[...], wo_ref[...], preferred_element_type=F32)
    y_ref[...] = _ffn(x1, g_ref, wg_ref, wu_ref, wd_ref, act_ref)


def _attn_out_ffn_call(x, o, wo, g, wg, wu, wd, tm):
    T, D = x.shape
    f = wg.shape[1]
    row_spec = pl.BlockSpec((tm, D), lambda i: (i, 0))
    return pl.pallas_call(
        _attn_out_ffn_kernel,
        grid=(T // tm,),
        in_specs=[row_spec, row_spec, _const_spec((D, D)), _const_spec((1, D)),
                  _const_spec((D, f)), _const_spec((D, f)), _const_spec((f, D))],
        out_specs=row_spec,
        out_shape=jax.ShapeDtypeStruct((T, D), F32),
        scratch_shapes=[pltpu.VMEM((tm, f), BF16)],
        compiler_params=pltpu.CompilerParams(
            dimension_semantics=("parallel",),
            vmem_limit_bytes=VMEM_LIMIT),
        name="attn_out_ffn",
    )(x, o, wo, g, wg, wu, wd)


def _conv_ffn_kernel(xm_ref, xp_ref, xn_ref, cg_ref, win_ref, bin_ref, dww_ref, dwb_ref,
                     lng_ref, lnb_ref, wout_ref, bout_ref, fg_ref, wg_ref, wu_ref, wd_ref,
                     fin_ref, y_ref, u_ref, c_ref, x3_ref, act_ref, *, rc, tiles_per_seq):
    ts, D = xm_ref.shape[1], xm_ref.shape[2]
    i = pl.program_id(0)
    tile = jnp.minimum(i, pl.num_programs(0) - 2)
    pos = tile % tiles_per_seq

    @pl.when(i == 0)
    def _():
        x3_ref[1] = jnp.zeros((ts, D), F32)

    xm = xm_ref[0]
    xe = jnp.concatenate([xp_ref[0], xm, xn_ref[0]], axis=0)
    h = _rms(xe, cg_ref[...]).astype(BF16)
    u = jnp.dot(h, win_ref[...], preferred_element_type=F32) + bin_ref[...]
    u = u[:, :D] * jax.nn.sigmoid(u[:, D:])
    row = lax.broadcasted_iota(jnp.int32, (ts + 2 * HALO, 1), 0)
    valid = jnp.logical_and(jnp.logical_or(row >= HALO, pos > 0),
                            jnp.logical_or(row < ts + HALO, pos < tiles_per_seq - 1))
    u_ref[...] = jnp.where(valid, u, 0.0)

    x4 = _ffn(x3_ref[(i + 1) % 2], fg_ref, wg_ref, wu_ref, wd_ref, act_ref)
    y_ref[0] = _rms(x4, fin_ref[...])

    off = HALO - CONV_PAD
    win_rows = rc + 2 * HALO
    for r in range(ts // rc):
        for c in range(D // LANES):
            cols = slice(c * LANES, (c + 1) * LANES)
            window = u_ref[r * rc:r * rc + win_rows, cols]
            acc = jnp.broadcast_to(dwb_ref[:, cols], (rc, LANES))
            for phase in range(SUBLANES):
                shifted = window if phase == 0 else pltpu.roll(window, win_rows - phase, 0)
                for base in range(0, 2 * HALO, SUBLANES):
                    k = base + phase - off
                    if 0 <= k < CONV_WIDTH:
                        acc = acc + shifted[base:base + rc] * dww_ref[k:k + 1, cols]
            c_ref[r * rc:(r + 1) * rc, cols] = acc
    v = c_ref[...]
    mu = jnp.mean(v, axis=-1, keepdims=True)
    vc = v - mu
    var = jnp.mean(vc * vc, axis=-1, keepdims=True)
    v = vc * lax.rsqrt(var + LN_EPS) * lng_ref[...] + lnb_ref[...]
    v = (v * jax.nn.sigmoid(v)).astype(BF16)
    x3_ref[i % 2] = xm + jnp.dot(v, wout_ref[...], preferred_element_type=F32) + bout_ref[...]


def _conv_ffn_call(x, cg, win, bin_, dww, dwb, lng, lnb, wout, bout, fg, wg, wu, wd, fin, ts, rc):
    B, S, D = x.shape
    f = wg.shape[1]
    hb = ts // HALO
    n_hb = S // HALO
    n_seq = S // ts
    n_tiles = B * n_seq

    def conv_tile(i):
        t = jnp.minimum(i, n_tiles - 1)
        return t // n_seq, t % n_seq

    def main_map(i):
        b, pos = conv_tile(i)
        return b, pos, 0

    def prev_halo_map(i):
        b, pos = conv_tile(i)
        return b, jnp.maximum(pos * hb - 1, 0), 0

    def next_halo_map(i):
        b, pos = conv_tile(i)
        return b, jnp.minimum((pos + 1) * hb, n_hb - 1), 0

    def out_map(i):
        t = jnp.maximum(i - 1, 0)
        return t // n_seq, t % n_seq, 0

    return pl.pallas_call(
        functools.partial(_conv_ffn_kernel, rc=rc, tiles_per_seq=n_seq),
        grid=(n_tiles + 1,),
        in_specs=[
            pl.BlockSpec((1, ts, D), main_map),
            pl.BlockSpec((1, HALO, D), prev_halo_map),
            pl.BlockSpec((1, HALO, D), next_halo_map),
            _const_spec((1, D)), _const_spec((D, 2 * D)), _const_spec((1, 2 * D)),
            _const_spec(dww.shape), _const_spec((1, D)), _const_spec((1, D)), _const_spec((1, D)),
            _const_spec((D, D)), _const_spec((1, D)), _const_spec((1, D)),
            _const_spec((D, f)), _const_spec((D, f)), _const_spec((f, D)), _const_spec((1, D)),
        ],
        out_specs=pl.BlockSpec((1, ts, D), out_map),
        out_shape=jax.ShapeDtypeStruct((B, S, D), F32),
        scratch_shapes=[
            pltpu.VMEM((ts + 2 * HALO, D), F32),
            pltpu.VMEM((ts, D), F32),
            pltpu.VMEM((2, ts, D), F32),
            pltpu.VMEM((ts, f), BF16),
        ],
        compiler_params=pltpu.CompilerParams(
            dimension_semantics=("arbitrary",),
            vmem_limit_bytes=VMEM_LIMIT),
        name="conv_ffn_final",
    )(x, x, x, cg, win, bin_, dww, dwb, lng, lnb, wout, bout, fg, wg, wu, wd, fin)


def _rope_tables(seq_len):
    t = jnp.arange(seq_len, dtype=jnp.int32)
    row_ids = (t // GRID_W).astype(F32)
    col_ids = (t % GRID_W).astype(F32)
    inv_freq = ROPE_THETA ** (-jnp.arange(0, ROPE_AXIS_DIM, 2, dtype=F32) / ROPE_AXIS_DIM)
    ang_r = row_ids[:, None] * inv_freq[None, :]
    ang_c = col_ids[:, None] * inv_freq[None, :]
    zero = jnp.zeros_like(ang_r)
    cos = jnp.concatenate([jnp.cos(ang_r)] * 2 + [jnp.cos(ang_c)] * 2, axis=-1)
    sa = jnp.concatenate([zero, jnp.sin(ang_r), zero, jnp.sin(ang_c)], axis=-1)
    sb = jnp.concatenate([-jnp.sin(ang_r), zero, -jnp.sin(ang_c), zero], axis=-1)
    reps = LANES // HEAD_DIM
    return tuple(jnp.tile(a, (1, reps)) for a in (cos, sa, sb))


def _trunk(x, p, tables):
    B, S, D = x.shape
    cos, sa, sb = (t[:S] for t in tables)
    qT, k, vT = _qkv_call(x, p["attn_g"], p["w_qkv"], p["gqk"], p["seg"], cos, sa, sb,
                          tm=512, tq=FLASH_TQ)
    o = lax.cond(p["score_bound"] <= SCORE_BOUND_MAX,
                 functools.partial(_flash_call, tq=FLASH_TQ, tk=FLASH_TK, online=False),
                 functools.partial(_flash_call, tq=FLASH_TQ, tk=FLASH_TK, online=True),
                 qT, k, vT)
    x2 = _attn_out_ffn_call(x.reshape(B * S, D), o.reshape(B * S, D), p["w_o"], p["ffn_g0"],
                            p["wg0"], p["wu0"], p["wd0"], tm=512)
    return _conv_ffn_call(x2.reshape(B, S, D), p["conv_g"], p["w_in"], p["b_in"], p["dw_w"],
                          p["dw_b"], p["ln_g"], p["ln_b"], p["w_out"], p["b_out"], p["ffn_g1"],
                          p["wg1"], p["wu1"], p["wd1"], p["fin_g"], ts=512, rc=64)


def kernel(x_prompt, x_sample, attn_norm_g, w_qkv, q_norm_g, k_norm_g, w_o, conv_norm_g, conv_w_in, conv_b_in, dw_w, dw_b, conv_ln_g, conv_ln_b, conv_w_out, conv_b_out, ffn_norm_g, w_gate, w_up, w_down, final_norm_g):
    row = lambda v: v.reshape(1, -1).astype(F32)
    scale = math.log2(math.e) / math.sqrt(HEAD_DIM)
    score_bound = (BF16_ROUNDING_MARGIN * HEAD_DIM * scale
                   * jnp.max(jnp.abs(q_norm_g[0])) * jnp.max(jnp.abs(k_norm_g[0])))
    gqk = jnp.concatenate([jnp.tile(q_norm_g[0] * scale, N_HEADS),
                           jnp.tile(k_norm_g[0], N_KV_HEADS)]).reshape(1, -1)
    lane = jnp.arange(MXU_N) // HEAD_DIM
    seg = jnp.where(lane[:, None] == lane[None, :], 1.0 / HEAD_DIM, 0.0).astype(BF16)
    p = dict(
        attn_g=row(attn_norm_g[0]), w_qkv=w_qkv[0].astype(BF16), gqk=gqk, seg=seg,
        w_o=w_o[0].astype(BF16), ffn_g0=row(ffn_norm_g[0]),
        wg0=w_gate[0].astype(BF16), wu0=w_up[0].astype(BF16), wd0=w_down[0].astype(BF16),
        conv_g=row(conv_norm_g[0]), w_in=conv_w_in[0].astype(BF16), b_in=row(conv_b_in[0]),
        dw_w=jnp.pad(dw_w[0], ((0, 1), (0, 0))), dw_b=row(dw_b[0]),
        ln_g=row(conv_ln_g[0]), ln_b=row(conv_ln_b[0]),
        w_out=conv_w_out[0].astype(BF16), b_out=row(conv_b_out[0]), ffn_g1=row(ffn_norm_g[1]),
        wg1=w_gate[1].astype(BF16), wu1=w_up[1].astype(BF16), wd1=w_down[1].astype(BF16),
        fin_g=row(final_norm_g), score_bound=score_bound,
    )
    tables = _rope_tables(max(x_prompt.shape[1], x_sample.shape[1]))
    return (_trunk(x_prompt, p, tables), _trunk(x_sample, p, tables))
```

```python
import functools
import math

import jax
import jax.numpy as jnp
from jax import lax
from jax.experimental import pallas as pl
from jax.experimental.pallas import tpu as pltpu

HEAD_DIM = 64
N_HEADS = 16
N_KV_HEADS = 4
GROUP = N_HEADS // N_KV_HEADS
GRID_W = 64
ROPE_AXIS_DIM = HEAD_DIM // 2
ROPE_HALF = ROPE_AXIS_DIM // 2
ROPE_THETA = 10000.0
CONV_WIDTH = 31
CONV_PAD = CONV_WIDTH // 2
RMS_EPS = 1e-6
LN_EPS = 1e-5

LANES = 128
SUBLANES = 8
MXU_N = 256
HALO = 16
BF16_SUBLANES = 16
V_ROWS = HEAD_DIM + BF16_SUBLANES
VMEM_LIMIT = 56 * 1024 * 1024

F32 = jnp.float32
BF16 = jnp.bfloat16


def _const_spec(shape):
    zeros = (0,) * len(shape)
    return pl.BlockSpec(shape, lambda *_: zeros, pipeline_mode=pl.Buffered(1))


def _rms(x, g):
    ms = jnp.mean(x * x, axis=-1, keepdims=True)
    return x * lax.rsqrt(ms + RMS_EPS) * g


def _qkv_kernel(x_ref, g_ref, w_ref, gqk_ref, seg_ref, cos_ref, sa_ref, sb_ref,
                qT_ref, k_ref, vT_ref, *, tq):
    tm = x_ref.shape[1]
    q_dim = N_HEADS * HEAD_DIM
    qk_dim = q_dim + N_KV_HEADS * HEAD_DIM
    h = _rms(x_ref[0], g_ref[...]).astype(BF16)

    def project(c):
        return jnp.dot(h, w_ref[:, c * MXU_N:(c + 1) * MXU_N], preferred_element_type=F32)

    seg = seg_ref[...]
    cos, sa, sb = cos_ref[...], sa_ref[...], sb_ref[...]
    z_next = project(0)
    for c in range(qk_dim // MXU_N):
        z, z_next = z_next, project(c + 1)
        y = z * z
        y_hi = y.astype(BF16)
        y_lo = (y - y_hi.astype(F32)).astype(BF16)
        ms = (jnp.dot(y_hi, seg, preferred_element_type=F32)
              + jnp.dot(y_lo, seg, preferred_element_type=F32))
        zn = z * lax.rsqrt(ms + RMS_EPS) * gqk_ref[:, c * MXU_N:(c + 1) * MXU_N]
        for half in range(MXU_N // LANES):
            zz = zn[:, half * LANES:(half + 1) * LANES]
            r = (zz * cos
                 + pltpu.roll(zz, ROPE_HALF, 1) * sa
                 + pltpu.roll(zz, LANES - ROPE_HALF, 1) * sb)
            col = c * MXU_N + half * LANES
            if col < q_dim:
                rT = r.T.astype(BF16)
                for hh in range(LANES // HEAD_DIM):
                    g = half * (LANES // HEAD_DIM) + hh
                    for tt in range(tm // tq):
                        dst = (tt * GROUP + g) * tq
                        qT_ref[0, c, :, dst:dst + tq] = rT[hh * HEAD_DIM:(hh + 1) * HEAD_DIM,
                                                           tt * tq:(tt + 1) * tq]
            else:
                kv = (col - q_dim) // HEAD_DIM
                k_ref[0, kv] = r[:, :HEAD_DIM].astype(BF16)
                k_ref[0, kv + 1] = r[:, HEAD_DIM:].astype(BF16)
    assert w_ref.shape[1] - qk_dim == MXU_N
    vT = z_next.T.astype(BF16)
    ones_rows = (lax.broadcasted_iota(jnp.int32, (V_ROWS - HEAD_DIM, tm), 0) == 0).astype(BF16)
    for kv in range(N_KV_HEADS):
        vT_ref[0, kv, :HEAD_DIM, :] = vT[kv * HEAD_DIM:(kv + 1) * HEAD_DIM]
        vT_ref[0, kv, HEAD_DIM:, :] = ones_rows


def _qkv_call(x, g, w_qkv, gqk, seg, cos, sa, sb, tm, tq):
    B, S, D = x.shape
    n_s = S // tm
    qkv_dim = w_qkv.shape[1]
    tab_spec = pl.BlockSpec((tm, LANES), lambda b, i: (i, 0))
    return pl.pallas_call(
        functools.partial(_qkv_kernel, tq=tq),
        grid=(B, n_s),
        in_specs=[
            pl.BlockSpec((1, tm, D), lambda b, i: (b, i, 0)),
            _const_spec((1, D)),
            _const_spec((D, qkv_dim)),
            _const_spec((1, gqk.shape[1])),
            _const_spec((MXU_N, MXU_N)),
            tab_spec, tab_spec, tab_spec,
        ],
        out_specs=[
            pl.BlockSpec((1, N_KV_HEADS, HEAD_DIM, GROUP * tm), lambda b, i: (b, 0, 0, i)),
            pl.BlockSpec((1, N_KV_HEADS, tm, HEAD_DIM), lambda b, i: (b, 0, i, 0)),
            pl.BlockSpec((1, N_KV_HEADS, V_ROWS, tm), lambda b, i: (b, 0, 0, i)),
        ],
        out_shape=[
            jax.ShapeDtypeStruct((B, N_KV_HEADS, HEAD_DIM, GROUP * S), BF16),
            jax.ShapeDtypeStruct((B, N_KV_HEADS, S, HEAD_DIM), BF16),
            jax.ShapeDtypeStruct((B, N_KV_HEADS, V_ROWS, S), BF16),
        ],
        compiler_params=pltpu.CompilerParams(
            dimension_semantics=("parallel", "parallel"),
            vmem_limit_bytes=VMEM_LIMIT),
        name="qkv_rope",
    )(x, g, w_qkv, gqk, seg, cos, sa, sb)


NEG_BIG = -1e30
SCORE_BOUND_MAX = 60.0
BF16_ROUNDING_MARGIN = 1.02


S_SLOTS = 4
P_SLOTS = 2
LOOKAHEAD = 2
UNROLL = 8
FLASH_TQ = 512
FLASH_TK = 256


def _flash_kernel(qT_ref, k_ref, vT_ref, o_ref, s_ref, p_ref, mx_ref, m_ref, acc_ref,
                  *, tk, tq):
    S = k_ref.shape[2]
    n = S // tk

    def scores(c, c_static):
        start = pl.multiple_of(c * tk, tk)
        s = jnp.dot(k_ref[0, 0, pl.ds(start, tk), :], qT_ref[0, 0],
                    preferred_element_type=F32)
        s_ref[c_static % S_SLOTS] = s
        mx_ref[c_static % S_SLOTS] = jnp.max(s, axis=0, keepdims=True)

    def pv(c, c_static):
        start = pl.multiple_of(c * tk, tk)
        return jnp.dot(vT_ref[0, 0, :, pl.ds(start, tk)], p_ref[c_static % P_SLOTS],
                       preferred_element_type=F32)

    def step(base, u, prefetch):
        i = base + u
        if prefetch:
            scores(i + LOOKAHEAD, u + LOOKAHEAD)
        pv_prev = pv(jnp.maximum(i - 1, 0), u - 1)
        m_old = m_ref[...]
        m_new = jnp.maximum(m_old, mx_ref[u % S_SLOTS])
        alpha = jnp.exp2(m_old - m_new)
        p_ref[u % P_SLOTS] = jnp.exp2(s_ref[u % S_SLOTS] - m_new).astype(BF16)
        m_ref[...] = m_new
        acc_ref[...] = (acc_ref[...] + pv_prev) * alpha

    m_ref[...] = jnp.full(m_ref.shape, NEG_BIG, F32)
    acc_ref[...] = jnp.zeros(acc_ref.shape, F32)
    p_ref[(-1) % P_SLOTS] = jnp.zeros(p_ref.shape[1:], BF16)
    for c in range(LOOKAHEAD):
        scores(c, c)

    def group(j, carry):
        for u in range(UNROLL):
            step(j * UNROLL, u, True)
        return carry

    lax.fori_loop(0, n // UNROLL - 1, group, 0)
    for u in range(UNROLL):
        step(n - UNROLL, u, u + LOOKAHEAD < UNROLL)
    acc = acc_ref[...] + pv(n - 1, n - 1)
    _flash_finish(acc[:HEAD_DIM], acc[HEAD_DIM:HEAD_DIM + 1], o_ref, tq)


def _flash_finish(o_t, denom, o_ref, tq):
    o = o_t * (1.0 / denom)
    o_ref[0] = jnp.concatenate(
        [o[:, g * tq:(g + 1) * tq].T for g in range(GROUP)], axis=1).astype(BF16)


def _flash_bounded_kernel(qT_ref, k_ref, vT_ref, o_ref, p_ref, l_ref, acc_ref, *, tk, tq):
    S = k_ref.shape[2]
    n = S // tk

    def numerators(c, c_static):
        start = pl.multiple_of(c * tk, tk)
        s = jnp.dot(k_ref[0, 0, pl.ds(start, tk), :], qT_ref[0, 0],
                    preferred_element_type=F32)
        p = jnp.exp2(s)
        l_ref[...] += sum(p[j * SUBLANES:(j + 1) * SUBLANES] for j in range(tk // SUBLANES))
        p_ref[c_static % S_SLOTS] = p.astype(BF16)

    def step(base, u, prefetch):
        i = base + u
        if prefetch:
            numerators(i + LOOKAHEAD, u + LOOKAHEAD)
        start = pl.multiple_of(i * tk, tk)
        acc_ref[...] += jnp.dot(vT_ref[0, 0, :HEAD_DIM, pl.ds(start, tk)], p_ref[u % S_SLOTS],
                                preferred_element_type=F32)

    acc_ref[...] = jnp.zeros(acc_ref.shape, F32)
    l_ref[...] = jnp.zeros(l_ref.shape, F32)
    for c in range(LOOKAHEAD):
        numerators(c, c)

    def group(j, carry):
        for u in range(UNROLL):
            step(j * UNROLL, u, True)
        return carry

    lax.fori_loop(0, n // UNROLL - 1, group, 0)
    for u in range(UNROLL):
        step(n - UNROLL, u, u + LOOKAHEAD < UNROLL)
    _flash_finish(acc_ref[...], jnp.sum(l_ref[...], axis=0, keepdims=True), o_ref, tq)


def _flash_call(qT, k, vT, tq, tk, online):
    B, _, S, _ = k.shape
    n_cols = GROUP * tq
    assert S % (UNROLL * tk) == 0 and UNROLL % S_SLOTS == 0 and UNROLL % P_SLOTS == 0
    assert LOOKAHEAD < S_SLOTS
    if online:
        body = _flash_kernel
        scratch = [
            pltpu.VMEM((S_SLOTS, tk, n_cols), F32),
            pltpu.VMEM((P_SLOTS, tk, n_cols), BF16),
            pltpu.VMEM((S_SLOTS, 1, n_cols), F32),
            pltpu.VMEM((1, n_cols), F32),
            pltpu.VMEM((V_ROWS, n_cols), F32),
        ]
    else:
        body = _flash_bounded_kernel
        scratch = [
            pltpu.VMEM((S_SLOTS, tk, n_cols), BF16),
            pltpu.VMEM((SUBLANES, n_cols), F32),
            pltpu.VMEM((HEAD_DIM, n_cols), F32),
        ]
    return pl.pallas_call(
        functools.partial(body, tk=tk, tq=tq),
        grid=(B, N_KV_HEADS, S // tq),
        in_specs=[
            pl.BlockSpec((1, 1, HEAD_DIM, n_cols), lambda b, j, i: (b, j, 0, i)),
            pl.BlockSpec((1, 1, S, HEAD_DIM), lambda b, j, i: (b, j, 0, 0)),
            pl.BlockSpec((1, 1, V_ROWS, S), lambda b, j, i: (b, j, 0, 0)),
        ],
        out_specs=pl.BlockSpec((1, tq, GROUP * HEAD_DIM), lambda b, j, i: (b, i, j)),
        out_shape=jax.ShapeDtypeStruct((B, S, N_HEADS * HEAD_DIM), BF16),
        scratch_shapes=scratch,
        compiler_params=pltpu.CompilerParams(
            dimension_semantics=("parallel", "parallel", "parallel"),
            vmem_limit_bytes=VMEM_LIMIT),
        name="flash_online" if online else "flash_bounded",
    )(qT, k, vT)


def _swiglu_chunk(h, wg_ref, wu_ref, act_ref, cols):
    gate = jnp.dot(h, wg_ref[:, cols], preferred_element_type=F32)
    up = jnp.dot(h, wu_ref[:, cols], preferred_element_type=F32)
    act_ref[:, cols] = (gate * jax.nn.sigmoid(gate) * up).astype(BF16)


def _ffn(x, g_ref, wg_ref, wu_ref, wd_ref, act_ref):
    h = _rms(x, g_ref[...]).astype(BF16)
    f = wg_ref.shape[1]
    for c in range(f // MXU_N):
        _swiglu_chunk(h, wg_ref, wu_ref, act_ref, slice(c * MXU_N, (c + 1) * MXU_N))
    return x + jnp.dot(act_ref[...], wd_ref[...], preferred_element_type=F32)


def _attn_out_ffn_kernel(x_ref, o_ref, wo_ref, g_ref, wg_ref, wu_ref, wd_ref,
                         y_ref, act_ref):
    x1 = x_ref[...] + jnp.dot(o_ref[...], wo_ref[...], preferred_element_type=F32)
    y_ref[...] = _ffn(x1, g_ref, wg_ref, wu_ref, wd_ref, act_ref)


def _attn_out_ffn_call(x, o, wo, g, wg, wu, wd, tm):
    T, D = x.shape
    f = wg.shape[1]
    row_spec = pl.BlockSpec((tm, D), lambda i: (i, 0))
    return pl.pallas_call(
        _attn_out_ffn_kernel,
        grid=(T // tm,),
        in_specs=[row_spec, row_spec, _const_spec((D, D)), _const_spec((1, D)),
                  _const_spec((D, f)), _const_spec((D, f)), _const_spec((f, D))],
        out_specs=row_spec,
        out_shape=jax.ShapeDtypeStruct((T, D), F32),
        scratch_shapes=[pltpu.VMEM((tm, f), BF16)],
        compiler_params=pltpu.CompilerParams(
            dimension_semantics=("parallel",),
            vmem_limit_bytes=VMEM_LIMIT),
        name="attn_out_ffn",
    )(x, o, wo, g, wg, wu, wd)


def _conv_ffn_kernel(xm_ref, xp_ref, xn_ref, cg_ref, win_ref, bin_ref, dww_ref, dwb_ref,
                     lng_ref, lnb_ref, wout_ref, bout_ref, fg_ref, wg_ref, wu_ref, wd_ref,
                     fin_ref, y_ref, u_ref, c_ref, x3_ref, act_ref, h_ref, *, rc, tiles_per_seq):
    ts, D = xm_ref.shape[1], xm_ref.shape[2]
    i = pl.program_id(0)
    tile = jnp.minimum(i, pl.num_programs(0) - 2)
    pos = tile % tiles_per_seq

    @pl.when(i == 0)
    def _():
        x3_ref[1] = jnp.zeros((ts, D), F32)

    xm = xm_ref[0]
    xe = jnp.concatenate([xp_ref[0], xm, xn_ref[0]], axis=0)
    h = _rms(xe, cg_ref[...]).astype(BF16)
    u = jnp.dot(h, win_ref[...], preferred_element_type=F32) + bin_ref[...]
    u = u[:, :D] * jax.nn.sigmoid(u[:, D:])
    row = lax.broadcasted_iota(jnp.int32, (ts + 2 * HALO, 1), 0)
    valid = jnp.logical_and(jnp.logical_or(row >= HALO, pos > 0),
                            jnp.logical_or(row < ts + HALO, pos < tiles_per_seq - 1))
    u_ref[...] = jnp.where(valid, u, 0.0)

    off = HALO - CONV_PAD
    win_rows = rc + 2 * HALO
    n_groups = D // LANES
    n_blocks = (ts // rc) * n_groups
    n_chunks = wg_ref.shape[1] // MXU_N
    blocks_per_chunk = -(-n_blocks // n_chunks)
    h_ref[...] = _rms(x3_ref[(i + 1) % 2], fg_ref[...]).astype(BF16)

    def conv_block(b):
        row0 = pl.multiple_of((b // n_groups) * rc, rc)
        cols = pl.ds(pl.multiple_of((b % n_groups) * LANES, LANES), LANES)
        window = u_ref[pl.ds(row0, win_rows), cols]
        acc = jnp.broadcast_to(dwb_ref[:, cols], (rc, LANES))
        for phase in range(SUBLANES):
            shifted = window if phase == 0 else pltpu.roll(window, win_rows - phase, 0)
            for base in range(0, 2 * HALO, SUBLANES):
                k = base + phase - off
                if 0 <= k < CONV_WIDTH:
                    acc = acc + shifted[base:base + rc] * dww_ref[k:k + 1, cols]
        c_ref[pl.ds(row0, rc), cols] = acc

    def ffn_and_conv(j, carry):
        _swiglu_chunk(h_ref[...], wg_ref, wu_ref, act_ref,
                      pl.ds(pl.multiple_of(j * MXU_N, MXU_N), MXU_N))
        for t in range(blocks_per_chunk):
            conv_block(jnp.minimum(j * blocks_per_chunk + t, n_blocks - 1))
        return carry

    lax.fori_loop(0, n_chunks, ffn_and_conv, 0)
    x4 = x3_ref[(i + 1) % 2] + jnp.dot(act_ref[...], wd_ref[...], preferred_element_type=F32)
    y_ref[0] = _rms(x4, fin_ref[...])

    v = c_ref[...]
    mu = jnp.mean(v, axis=-1, keepdims=True)
    vc = v - mu
    var = jnp.mean(vc * vc, axis=-1, keepdims=True)
    v = vc * lax.rsqrt(var + LN_EPS) * lng_ref[...] + lnb_ref[...]
    v = (v * jax.nn.sigmoid(v)).astype(BF16)
    x3_ref[i % 2] = xm + jnp.dot(v, wout_ref[...], preferred_element_type=F32) + bout_ref[...]


def _conv_ffn_call(x, cg, win, bin_, dww, dwb, lng, lnb, wout, bout, fg, wg, wu, wd, fin, ts, rc):
    B, S, D = x.shape
    f = wg.shape[1]
    hb = ts // HALO
    n_hb = S // HALO
    n_seq = S // ts
    n_tiles = B * n_seq

    def conv_tile(i):
        t = jnp.minimum(i, n_tiles - 1)
        return t // n_seq, t % n_seq

    def main_map(i):
        b, pos = conv_tile(i)
        return b, pos, 0

    def prev_halo_map(i):
        b, pos = conv_tile(i)
        return b, jnp.maximum(pos * hb - 1, 0), 0

    def next_halo_map(i):
        b, pos = conv_tile(i)
        return b, jnp.minimum((pos + 1) * hb, n_hb - 1), 0

    def out_map(i):
        t = jnp.maximum(i - 1, 0)
        return t // n_seq, t % n_seq, 0

    return pl.pallas_call(
        functools.partial(_conv_ffn_kernel, rc=rc, tiles_per_seq=n_seq),
        grid=(n_tiles + 1,),
        in_specs=[
            pl.BlockSpec((1, ts, D), main_map),
            pl.BlockSpec((1, HALO, D), prev_halo_map),
            pl.BlockSpec((1, HALO, D), next_halo_map),
            _const_spec((1, D)), _const_spec((D, 2 * D)), _const_spec((1, 2 * D)),
            _const_spec(dww.shape), _const_spec((1, D)), _const_spec((1, D)), _const_spec((1, D)),
            _const_spec((D, D)), _const_spec((1, D)), _const_spec((1, D)),
            _const_spec((D, f)), _const_spec((D, f)), _const_spec((f, D)), _const_spec((1, D)),
        ],
        out_specs=pl.BlockSpec((1, ts, D), out_map),
        out_shape=jax.ShapeDtypeStruct((B, S, D), F32),
        scratch_shapes=[
            pltpu.VMEM((ts + 2 * HALO, D), F32),
            pltpu.VMEM((ts, D), F32),
            pltpu.VMEM((2, ts, D), F32),
            pltpu.VMEM((ts, f), BF16),
            pltpu.VMEM((ts, D), BF16),
        ],
        compiler_params=pltpu.CompilerParams(
            dimension_semantics=("arbitrary",),
            vmem_limit_bytes=VMEM_LIMIT),
        name="conv_ffn_final",
    )(x, x, x, cg, win, bin_, dww, dwb, lng, lnb, wout, bout, fg, wg, wu, wd, fin)


def _rope_tables(seq_len):
    t = jnp.arange(seq_len, dtype=jnp.int32)
    row_ids = (t // GRID_W).astype(F32)
    col_ids = (t % GRID_W).astype(F32)
    inv_freq = ROPE_THETA ** (-jnp.arange(0, ROPE_AXIS_DIM, 2, dtype=F32) / ROPE_AXIS_DIM)
    ang_r = row_ids[:, None] * inv_freq[None, :]
    ang_c = col_ids[:, None] * inv_freq[None, :]
    zero = jnp.zeros_like(ang_r)
    cos = jnp.concatenate([jnp.cos(ang_r)] * 2 + [jnp.cos(ang_c)] * 2, axis=-1)
    sa = jnp.concatenate([zero, jnp.sin(ang_r), zero, jnp.sin(ang_c)], axis=-1)
    sb = jnp.concatenate([-jnp.sin(ang_r), zero, -jnp.sin(ang_c), zero], axis=-1)
    reps = LANES // HEAD_DIM
    return tuple(jnp.tile(a, (1, reps)) for a in (cos, sa, sb))


def _trunk(x, p, tables):
    B, S, D = x.shape
    cos, sa, sb = (t[:S] for t in tables)
    qT, k, vT = _qkv_call(x, p["attn_g"], p["w_qkv"], p["gqk"], p["seg"], cos, sa, sb,
                          tm=512, tq=FLASH_TQ)
    o = lax.cond(p["score_bound"] <= SCORE_BOUND_MAX,
                 functools.partial(_flash_call, tq=FLASH_TQ, tk=FLASH_TK, online=False),
                 functools.partial(_flash_call, tq=FLASH_TQ, tk=FLASH_TK, online=True),
                 qT, k, vT)
    x2 = _attn_out_ffn_call(x.reshape(B * S, D), o.reshape(B * S, D), p["w_o"], p["ffn_g0"],
                            p["wg0"], p["wu0"], p["wd0"], tm=512)
    return _conv_ffn_call(x2.reshape(B, S, D), p["conv_g"], p["w_in"], p["b_in"], p["dw_w"],
                          p["dw_b"], p["ln_g"], p["ln_b"], p["w_out"], p["b_out"], p["ffn_g1"],
                          p["wg1"], p["wu1"], p["wd1"], p["fin_g"], ts=512, rc=64)


def kernel(x_prompt, x_sample, attn_norm_g, w_qkv, q_norm_g, k_norm_g, w_o, conv_norm_g, conv_w_in, conv_b_in, dw_w, dw_b, conv_ln_g, conv_ln_b, conv_w_out, conv_b_out, ffn_norm_g, w_gate, w_up, w_down, final_norm_g):
    row = lambda v: v.reshape(1, -1).astype(F32)
    scale = math.log2(math.e) / math.sqrt(HEAD_DIM)
    score_bound = (BF16_ROUNDING_MARGIN * HEAD_DIM * scale
                   * jnp.max(jnp.abs(q_norm_g[0])) * jnp.max(jnp.abs(k_norm_g[0])))
    gqk = jnp.concatenate([jnp.tile(q_norm_g[0] * scale, N_HEADS),
                           jnp.tile(k_norm_g[0], N_KV_HEADS)]).reshape(1, -1)
    lane = jnp.arange(MXU_N) // HEAD_DIM
    seg = jnp.where(lane[:, None] == lane[None, :], 1.0 / HEAD_DIM, 0.0).astype(BF16)
    p = dict(
        attn_g=row(attn_norm_g[0]), w_qkv=w_qkv[0].astype(BF16), gqk=gqk, seg=seg,
        w_o=w_o[0].astype(BF16), ffn_g0=row(ffn_norm_g[0]),
        wg0=w_gate[0].astype(BF16), wu0=w_up[0].astype(BF16), wd0=w_down[0].astype(BF16),
        conv_g=row(conv_norm_g[0]), w_in=conv_w_in[0].astype(BF16), b_in=row(conv_b_in[0]),
        dw_w=jnp.pad(dw_w[0], ((0, 1), (0, 0))), dw_b=row(dw_b[0]),
        ln_g=row(conv_ln_g[0]), ln_b=row(conv_ln_b[0]),
        w_out=conv_w_out[0].astype(BF16), b_out=row(conv_b_out[0]), ffn_g1=row(ffn_norm_g[1]),
        wg1=w_gate[1].astype(BF16), wu1=w_up[1].astype(BF16), wd1=w_down[1].astype(BF16),
        fin_g=row(final_norm_g), score_bound=score_bound,
    )
    tables = _rope_tables(max(x_prompt.shape[1], x_sample.shape[1]))
    return (_trunk(x_prompt, p, tables), _trunk(x_sample, p, tables))
```

```python
import functools
import math

import jax
import jax.numpy as jnp
from jax import lax
from jax.experimental import pallas as pl
from jax.experimental.pallas import tpu as pltpu

HEAD_DIM = 64
N_HEADS = 16
N_KV_HEADS = 4
GROUP = N_HEADS // N_KV_HEADS
GRID_W = 64
ROPE_AXIS_DIM = HEAD_DIM // 2
ROPE_HALF = ROPE_AXIS_DIM // 2
ROPE_THETA = 10000.0
CONV_WIDTH = 31
CONV_PAD = CONV_WIDTH // 2
RMS_EPS = 1e-6
LN_EPS = 1e-5

LANES = 128
SUBLANES = 8
MXU_N = 256
HALO = 16
BF16_SUBLANES = 16
V_ROWS = HEAD_DIM + BF16_SUBLANES
VMEM_LIMIT = 56 * 1024 * 1024

F32 = jnp.float32
BF16 = jnp.bfloat16


def _const_spec(shape):
    zeros = (0,) * len(shape)
    return pl.BlockSpec(shape, lambda *_: zeros, pipeline_mode=pl.Buffered(1))


def _rms(x, g):
    ms = jnp.mean(x * x, axis=-1, keepdims=True)
    return x * lax.rsqrt(ms + RMS_EPS) * g


def _qkv_kernel(x_ref, g_ref, w_ref, gqk_ref, seg_ref, cos_ref, sa_ref, sb_ref,
                qT_ref, k_ref, vT_ref, *, tq):
    tm = x_ref.shape[1]
    q_dim = N_HEADS * HEAD_DIM
    qk_dim = q_dim + N_KV_HEADS * HEAD_DIM
    h = _rms(x_ref[0], g_ref[...]).astype(BF16)

    def project(c):
        return jnp.dot(h, w_ref[:, c * MXU_N:(c + 1) * MXU_N], preferred_element_type=F32)

    seg = seg_ref[...]
    cos, sa, sb = cos_ref[...], sa_ref[...], sb_ref[...]
    z_next = project(0)
    for c in range(qk_dim // MXU_N):
        z, z_next = z_next, project(c + 1)
        y = z * z
        y_hi = y.astype(BF16)
        y_lo = (y - y_hi.astype(F32)).astype(BF16)
        ms = (jnp.dot(y_hi, seg, preferred_element_type=F32)
              + jnp.dot(y_lo, seg, preferred_element_type=F32))
        zn = z * lax.rsqrt(ms + RMS_EPS) * gqk_ref[:, c * MXU_N:(c + 1) * MXU_N]
        for half in range(MXU_N // LANES):
            zz = zn[:, half * LANES:(half + 1) * LANES]
            r = (zz * cos
                 + pltpu.roll(zz, ROPE_HALF, 1) * sa
                 + pltpu.roll(zz, LANES - ROPE_HALF, 1) * sb)
            col = c * MXU_N + half * LANES
            if col < q_dim:
                rT = r.T.astype(BF16)
                for hh in range(LANES // HEAD_DIM):
                    g = half * (LANES // HEAD_DIM) + hh
                    for tt in range(tm // tq):
                        dst = (tt * GROUP + g) * tq
                        qT_ref[0, c, :, dst:dst + tq] = rT[hh * HEAD_DIM:(hh + 1) * HEAD_DIM,
                                                           tt * tq:(tt + 1) * tq]
            else:
                kv = (col - q_dim) // HEAD_DIM
                k_ref[0, kv] = r[:, :HEAD_DIM].astype(BF16)
                k_ref[0, kv + 1] = r[:, HEAD_DIM:].astype(BF16)
    assert w_ref.shape[1] - qk_dim == MXU_N
    vT = z_next.T.astype(BF16)
    ones_rows = (lax.broadcasted_iota(jnp.int32, (V_ROWS - HEAD_DIM, tm), 0) == 0).astype(BF16)
    for kv in range(N_KV_HEADS):
        vT_ref[0, kv, :HEAD_DIM, :] = vT[kv * HEAD_DIM:(kv + 1) * HEAD_DIM]
        vT_ref[0, kv, HEAD_DIM:, :] = ones_rows


def _qkv_call(x, g, w_qkv, gqk, seg, cos, sa, sb, tm, tq):
    B, S, D = x.shape
    n_s = S // tm
    qkv_dim = w_qkv.shape[1]
    tab_spec = pl.BlockSpec((tm, LANES), lambda b, i: (i, 0))
    return pl.pallas_call(
        functools.partial(_qkv_kernel, tq=tq),
        grid=(B, n_s),
        in_specs=[
            pl.BlockSpec((1, tm, D), lambda b, i: (b, i, 0)),
            _const_spec((1, D)),
            _const_spec((D, qkv_dim)),
            _const_spec((1, gqk.shape[1])),
            _const_spec((MXU_N, MXU_N)),
            tab_spec, tab_spec, tab_spec,
        ],
        out_specs=[
            pl.BlockSpec((1, N_KV_HEADS, HEAD_DIM, GROUP * tm), lambda b, i: (b, 0, 0, i)),
            pl.BlockSpec((1, N_KV_HEADS, tm, HEAD_DIM), lambda b, i: (b, 0, i, 0)),
            pl.BlockSpec((1, N_KV_HEADS, V_ROWS, tm), lambda b, i: (b, 0, 0, i)),
        ],
        out_shape=[
            jax.ShapeDtypeStruct((B, N_KV_HEADS, HEAD_DIM, GROUP * S), BF16),
            jax.ShapeDtypeStruct((B, N_KV_HEADS, S, HEAD_DIM), BF16),
            jax.ShapeDtypeStruct((B, N_KV_HEADS, V_ROWS, S), BF16),
        ],
        compiler_params=pltpu.CompilerParams(
            dimension_semantics=("parallel", "parallel"),
            vmem_limit_bytes=VMEM_LIMIT),
        name="qkv_rope",
    )(x, g, w_qkv, gqk, seg, cos, sa, sb)


NEG_BIG = -1e30
SCORE_BOUND_MAX = 60.0
BF16_ROUNDING_MARGIN = 1.02


S_SLOTS = 4
P_SLOTS = 2
LOOKAHEAD = 2
UNROLL = 8
FLASH_TQ = 512
FLASH_TK = 256


def _flash_kernel(qT_ref, k_ref, vT_ref, o_ref, s_ref, p_ref, mx_ref, m_ref, acc_ref,
                  *, tk, tq):
    S = k_ref.shape[2]
    n = S // tk

    def scores(c, c_static):
        start = pl.multiple_of(c * tk, tk)
        s = jnp.dot(k_ref[0, 0, pl.ds(start, tk), :], qT_ref[0, 0],
                    preferred_element_type=F32)
        s_ref[c_static % S_SLOTS] = s
        mx_ref[c_static % S_SLOTS] = jnp.max(s, axis=0, keepdims=True)

    def pv(c, c_static):
        start = pl.multiple_of(c * tk, tk)
        return jnp.dot(vT_ref[0, 0, :, pl.ds(start, tk)], p_ref[c_static % P_SLOTS],
                       preferred_element_type=F32)

    def step(base, u, prefetch):
        i = base + u
        if prefetch:
            scores(i + LOOKAHEAD, u + LOOKAHEAD)
        pv_prev = pv(jnp.maximum(i - 1, 0), u - 1)
        m_old = m_ref[...]
        m_new = jnp.maximum(m_old, mx_ref[u % S_SLOTS])
        alpha = jnp.exp2(m_old - m_new)
        p_ref[u % P_SLOTS] = jnp.exp2(s_ref[u % S_SLOTS] - m_new).astype(BF16)
        m_ref[...] = m_new
        acc_ref[...] = (acc_ref[...] + pv_prev) * alpha

    m_ref[...] = jnp.full(m_ref.shape, NEG_BIG, F32)
    acc_ref[...] = jnp.zeros(acc_ref.shape, F32)
    p_ref[(-1) % P_SLOTS] = jnp.zeros(p_ref.shape[1:], BF16)
    for c in range(LOOKAHEAD):
        scores(c, c)

    def group(j, carry):
        for u in range(UNROLL):
            step(j * UNROLL, u, True)
        return carry

    lax.fori_loop(0, n // UNROLL - 1, group, 0)
    for u in range(UNROLL):
        step(n - UNROLL, u, u + LOOKAHEAD < UNROLL)
    acc = acc_ref[...] + pv(n - 1, n - 1)
    _flash_finish(acc[:HEAD_DIM], acc[HEAD_DIM:HEAD_DIM + 1], o_ref, tq)


def _flash_finish(o_t, denom, o_ref, tq):
    o = o_t * (1.0 / denom)
    o_ref[0] = jnp.concatenate(
        [o[:, g * tq:(g + 1) * tq].T for g in range(GROUP)], axis=1).astype(BF16)


def _flash_bounded_kernel(qT_ref, k_ref, vT_ref, o_ref, p_ref, l_ref, acc_ref, *, tk, tq):
    S = k_ref.shape[2]
    n = S // tk

    def numerators(c, c_static):
        start = pl.multiple_of(c * tk, tk)
        s = jnp.dot(k_ref[0, 0, pl.ds(start, tk), :], qT_ref[0, 0],
                    preferred_element_type=F32)
        p = jnp.exp2(s)
        l_ref[...] += sum(p[j * SUBLANES:(j + 1) * SUBLANES] for j in range(tk // SUBLANES))
        p_ref[c_static % S_SLOTS] = p.astype(BF16)

    def step(base, u, prefetch):
        i = base + u
        if prefetch:
            numerators(i + LOOKAHEAD, u + LOOKAHEAD)
        start = pl.multiple_of(i * tk, tk)
        acc_ref[...] += jnp.dot(vT_ref[0, 0, :HEAD_DIM, pl.ds(start, tk)], p_ref[u % S_SLOTS],
                                preferred_element_type=F32)

    acc_ref[...] = jnp.zeros(acc_ref.shape, F32)
    l_ref[...] = jnp.zeros(l_ref.shape, F32)
    for c in range(LOOKAHEAD):
        numerators(c, c)

    def group(j, carry):
        for u in range(UNROLL):
            step(j * UNROLL, u, True)
        return carry

    lax.fori_loop(0, n // UNROLL - 1, group, 0)
    for u in range(UNROLL):
        step(n - UNROLL, u, u + LOOKAHEAD < UNROLL)
    _flash_finish(acc_ref[...], jnp.sum(l_ref[...], axis=0, keepdims=True), o_ref, tq)


def _flash_call(qT, k, vT, tq, tk, online):
    B, _, S, _ = k.shape
    n_cols = GROUP * tq
    assert S % (UNROLL * tk) == 0 and UNROLL % S_SLOTS == 0 and UNROLL % P_SLOTS == 0
    assert LOOKAHEAD < S_SLOTS
    if online:
        body = _flash_kernel
        scratch = [
            pltpu.VMEM((S_SLOTS, tk, n_cols), F32),
            pltpu.VMEM((P_SLOTS, tk, n_cols), BF16),
            pltpu.VMEM((S_SLOTS, 1, n_cols), F32),
            pltpu.VMEM((1, n_cols), F32),
            pltpu.VMEM((V_ROWS, n_cols), F32),
        ]
    else:
        body = _flash_bounded_kernel
        scratch = [
            pltpu.VMEM((S_SLOTS, tk, n_cols), BF16),
            pltpu.VMEM((SUBLANES, n_cols), F32),
            pltpu.VMEM((HEAD_DIM, n_cols), F32),
        ]
    return pl.pallas_call(
        functools.partial(body, tk=tk, tq=tq),
        grid=(B, N_KV_HEADS, S // tq),
        in_specs=[
            pl.BlockSpec((1, 1, HEAD_DIM, n_cols), lambda b, j, i: (b, j, 0, i)),
            pl.BlockSpec((1, 1, S, HEAD_DIM), lambda b, j, i: (b, j, 0, 0)),
            pl.BlockSpec((1, 1, V_ROWS, S), lambda b, j, i: (b, j, 0, 0)),
        ],
        out_specs=pl.BlockSpec((1, tq, GROUP * HEAD_DIM), lambda b, j, i: (b, i, j)),
        out_shape=jax.ShapeDtypeStruct((B, S, N_HEADS * HEAD_DIM), BF16),
        scratch_shapes=scratch,
        compiler_params=pltpu.CompilerParams(
            dimension_semantics=("parallel", "parallel", "parallel"),
            vmem_limit_bytes=VMEM_LIMIT),
        name="flash_online" if online else "flash_bounded",
    )(qT, k, vT)


def _swiglu_chunk(h, wg_ref, wu_ref, act_ref, cols):
    gate = jnp.dot(h, wg_ref[:, cols], preferred_element_type=F32)
    up = jnp.dot(h, wu_ref[:, cols], preferred_element_type=F32)
    act_ref[:, cols] = (gate * jax.nn.sigmoid(gate) * up).astype(BF16)


def _ffn(x, g_ref, wg_ref, wu_ref, wd_ref, act_ref):
    h = _rms(x, g_ref[...]).astype(BF16)
    f = wg_ref.shape[1]
    for c in range(f // MXU_N):
        _swiglu_chunk(h, wg_ref, wu_ref, act_ref, slice(c * MXU_N, (c + 1) * MXU_N))
    return x + jnp.dot(act_ref[...], wd_ref[...], preferred_element_type=F32)


def _attn_out_ffn_kernel(x_ref, o_ref, wo_ref, g_ref, wg_ref, wu_ref, wd_ref,
                         y_ref, act_ref):
    x1 = x_ref[...] + jnp.dot(o_ref[...], wo_ref[...], preferred_element_type=F32)
    y_ref[...] = _ffn(x1, g_ref, wg_ref, wu_ref, wd_ref, act_ref)


def _attn_out_ffn_call(x, o, wo, g, wg, wu, wd, tm):
    T, D = x.shape
    f = wg.shape[1]
    row_spec = pl.BlockSpec((tm, D), lambda i: (i, 0))
    return pl.pallas_call(
        _attn_out_ffn_kernel,
        grid=(T // tm,),
        in_specs=[row_spec, row_spec, _const_spec((D, D)), _const_spec((1, D)),
                  _const_spec((D, f)), _const_spec((D, f)), _const_spec((f, D))],
        out_specs=row_spec,
        out_shape=jax.ShapeDtypeStruct((T, D), F32),
        scratch_shapes=[pltpu.VMEM((tm, f), BF16)],
        compiler_params=pltpu.CompilerParams(
            dimension_semantics=("parallel",),
            vmem_limit_bytes=VMEM_LIMIT),
        name="attn_out_ffn",
    )(x, o, wo, g, wg, wu, wd)


def _conv_ffn_kernel(xm_ref, xp_ref, xn_ref, cg_ref, win_ref, bin_ref, dww_ref, dwb_ref,
                     lng_ref, lnb_ref, wout_ref, bout_ref, fg_ref, wg_ref, wu_ref, wd_ref,
                     fin_ref, y_ref, u_ref, c_ref, x3_ref, act_ref, h_ref, *, rc, tiles_per_seq):
    ts, D = xm_ref.shape[1], xm_ref.shape[2]
    i = pl.program_id(0)
    tile = jnp.minimum(i, pl.num_programs(0) - 2)
    pos = tile % tiles_per_seq

    @pl.when(i == 0)
    def _():
        x3_ref[1] = jnp.zeros((ts, D), F32)

    xm = xm_ref[0]
    xe = jnp.concatenate([xp_ref[0], xm, xn_ref[0]], axis=0)
    h = _rms(xe, cg_ref[...]).astype(BF16)
    u = jnp.dot(h, win_ref[...], preferred_element_type=F32) + bin_ref[...]
    u = u[:, :D] * jax.nn.sigmoid(u[:, D:])
    row = lax.broadcasted_iota(jnp.int32, (ts + 2 * HALO, 1), 0)
    valid = jnp.logical_and(jnp.logical_or(row >= HALO, pos > 0),
                            jnp.logical_or(row < ts + HALO, pos < tiles_per_seq - 1))
    u_ref[...] = jnp.where(valid, u, 0.0)

    off = HALO - CONV_PAD
    win_rows = rc + 2 * HALO
    n_groups = D // LANES
    n_blocks = (ts // rc) * n_groups
    n_chunks = wg_ref.shape[1] // MXU_N
    blocks_per_chunk = -(-n_blocks // n_chunks)
    h_ref[...] = _rms(x3_ref[(i + 1) % 2], fg_ref[...]).astype(BF16)

    def conv_block(b):
        row0 = (b // n_groups) * rc
        cols = slice((b % n_groups) * LANES, (b % n_groups + 1) * LANES)
        window = u_ref[row0:row0 + win_rows, cols]
        acc = jnp.broadcast_to(dwb_ref[:, cols], (rc, LANES))
        for phase in range(SUBLANES):
            shifted = window if phase == 0 else pltpu.roll(window, win_rows - phase, 0)
            for base in range(0, 2 * HALO, SUBLANES):
                k = base + phase - off
                if 0 <= k < CONV_WIDTH:
                    acc = acc + shifted[base:base + rc] * dww_ref[k:k + 1, cols]
        c_ref[row0:row0 + rc, cols] = acc

    for j in range(n_chunks):
        @pl.when(i >= 0)
        def _(j=j):
            _swiglu_chunk(h_ref[...], wg_ref, wu_ref, act_ref,
                          slice(j * MXU_N, (j + 1) * MXU_N))
            for b in range(j * blocks_per_chunk, min((j + 1) * blocks_per_chunk, n_blocks)):
                conv_block(b)

    x4 = x3_ref[(i + 1) % 2] + jnp.dot(act_ref[...], wd_ref[...], preferred_element_type=F32)
    y_ref[0] = _rms(x4, fin_ref[...])

    v = c_ref[...]
    mu = jnp.mean(v, axis=-1, keepdims=True)
    vc = v - mu
    var = jnp.mean(vc * vc, axis=-1, keepdims=True)
    v = vc * lax.rsqrt(var + LN_EPS) * lng_ref[...] + lnb_ref[...]
    v = (v * jax.nn.sigmoid(v)).astype(BF16)
    x3_ref[i % 2] = xm + jnp.dot(v, wout_ref[...], preferred_element_type=F32) + bout_ref[...]


def _conv_ffn_call(x, cg, win, bin_, dww, dwb, lng, lnb, wout, bout, fg, wg, wu, wd, fin, ts, rc):
    B, S, D = x.shape
    f = wg.shape[1]
    hb = ts // HALO
    n_hb = S // HALO
    n_seq = S // ts
    n_tiles = B * n_seq

    def conv_tile(i):
        t = jnp.minimum(i, n_tiles - 1)
        return t // n_seq, t % n_seq

    def main_map(i):
        b, pos = conv_tile(i)
        return b, pos, 0

    def prev_halo_map(i):
        b, pos = conv_tile(i)
        return b, jnp.maximum(pos * hb - 1, 0), 0

    def next_halo_map(i):
        b, pos = conv_tile(i)
        return b, jnp.minimum((pos + 1) * hb, n_hb - 1), 0

    def out_map(i):
        t = jnp.maximum(i - 1, 0)
        return t // n_seq, t % n_seq, 0

    return pl.pallas_call(
        functools.partial(_conv_ffn_kernel, rc=rc, tiles_per_seq=n_seq),
        grid=(n_tiles + 1,),
        in_specs=[
            pl.BlockSpec((1, ts, D), main_map),
            pl.BlockSpec((1, HALO, D), prev_halo_map),
            pl.BlockSpec((1, HALO, D), next_halo_map),
            _const_spec((1, D)), _const_spec((D, 2 * D)), _const_spec((1, 2 * D)),
            _const_spec(dww.shape), _const_spec((1, D)), _const_spec((1, D)), _const_spec((1, D)),
            _const_spec((D, D)), _const_spec((1, D)), _const_spec((1, D)),
            _const_spec((D, f)), _const_spec((D, f)), _const_spec((f, D)), _const_spec((1, D)),
        ],
        out_specs=pl.BlockSpec((1, ts, D), out_map),
        out_shape=jax.ShapeDtypeStruct((B, S, D), F32),
        scratch_shapes=[
            pltpu.VMEM((ts + 2 * HALO, D), F32),
            pltpu.VMEM((ts, D), F32),
            pltpu.VMEM((2, ts, D), F32),
            pltpu.VMEM((ts, f), BF16),
            pltpu.VMEM((ts, D), BF16),
        ],
        compiler_params=pltpu.CompilerParams(
            dimension_semantics=("arbitrary",),
            vmem_limit_bytes=VMEM_LIMIT),
        name="conv_ffn_final",
    )(x, x, x, cg, win, bin_, dww, dwb, lng, lnb, wout, bout, fg, wg, wu, wd, fin)


def _rope_tables(seq_len):
    t = jnp.arange(seq_len, dtype=jnp.int32)
    row_ids = (t // GRID_W).astype(F32)
    col_ids = (t % GRID_W).astype(F32)
    inv_freq = ROPE_THETA ** (-jnp.arange(0, ROPE_AXIS_DIM, 2, dtype=F32) / ROPE_AXIS_DIM)
    ang_r = row_ids[:, None] * inv_freq[None, :]
    ang_c = col_ids[:, None] * inv_freq[None, :]
    zero = jnp.zeros_like(ang_r)
    cos = jnp.concatenate([jnp.cos(ang_r)] * 2 + [jnp.cos(ang_c)] * 2, axis=-1)
    sa = jnp.concatenate([zero, jnp.sin(ang_r), zero, jnp.sin(ang_c)], axis=-1)
    sb = jnp.concatenate([-jnp.sin(ang_r), zero, -jnp.sin(ang_c), zero], axis=-1)
    reps = LANES // HEAD_DIM
    return tuple(jnp.tile(a, (1, reps)) for a in (cos, sa, sb))


def _trunk(x, p, tables):
    B, S, D = x.shape
    cos, sa, sb = (t[:S] for t in tables)
    qT, k, vT = _qkv_call(x, p["attn_g"], p["w_qkv"], p["gqk"], p["seg"], cos, sa, sb,
                          tm=512, tq=FLASH_TQ)
    o = lax.cond(p["score_bound"] <= SCORE_BOUND_MAX,
                 functools.partial(_flash_call, tq=FLASH_TQ, tk=FLASH_TK, online=False),
                 functools.partial(_flash_call, tq=FLASH_TQ, tk=FLASH_TK, online=True),
                 qT, k, vT)
    x2 = _attn_out_ffn_call(x.reshape(B * S, D), o.reshape(B * S, D), p["w_o"], p["ffn_g0"],
                            p["wg0"], p["wu0"], p["wd0"], tm=512)
    return _conv_ffn_call(x2.reshape(B, S, D), p["conv_g"], p["w_in"], p["b_in"], p["dw_w"],
                          p["dw_b"], p["ln_g"], p["ln_b"], p["w_out"], p["b_out"], p["ffn_g1"],
                          p["wg1"], p["wu1"], p["wd1"], p["fin_g"], ts=512, rc=64)


def kernel(x_prompt, x_sample, attn_norm_g, w_qkv, q_norm_g, k_norm_g, w_o, conv_norm_g, conv_w_in, conv_b_in, dw_w, dw_b, conv_ln_g, conv_ln_b, conv_w_out, conv_b_out, ffn_norm_g, w_gate, w_up, w_down, final_norm_g):
    row = lambda v: v.reshape(1, -1).astype(F32)
    scale = math.log2(math.e) / math.sqrt(HEAD_DIM)
    score_bound = (BF16_ROUNDING_MARGIN * HEAD_DIM * scale
                   * jnp.max(jnp.abs(q_norm_g[0])) * jnp.max(jnp.abs(k_norm_g[0])))
    gqk = jnp.concatenate([jnp.tile(q_norm_g[0] * scale, N_HEADS),
                           jnp.tile(k_norm_g[0], N_KV_HEADS)]).reshape(1, -1)
    lane = jnp.arange(MXU_N) // HEAD_DIM
    seg = jnp.where(lane[:, None] == lane[None, :], 1.0 / HEAD_DIM, 0.0).astype(BF16)
    p = dict(
        attn_g=row(attn_norm_g[0]), w_qkv=w_qkv[0].astype(BF16), gqk=gqk, seg=seg,
        w_o=w_o[0].astype(BF16), ffn_g0=row(ffn_norm_g[0]),
        wg0=w_gate[0].astype(BF16), wu0=w_up[0].astype(BF16), wd0=w_down[0].astype(BF16),
        conv_g=row(conv_norm_g[0]), w_in=conv_w_in[0].astype(BF16), b_in=row(conv_b_in[0]),
        dw_w=jnp.pad(dw_w[0], ((0, 1), (0, 0))), dw_b=row(dw_b[0]),
        ln_g=row(conv_ln_g[0]), ln_b=row(conv_ln_b[0]),
        w_out=conv_w_out[0].astype(BF16), b_out=row(conv_b_out[0]), ffn_g1=row(ffn_norm_g[1]),
        wg1=w_gate[1].astype(BF16), wu1=w_up[1].astype(BF16), wd1=w_down[1].astype(BF16),
        fin_g=row(final_norm_g), score_bound=score_bound,
    )
    tables = _rope_tables(max(x_prompt.shape[1], x_sample.shape[1]))
    return (_trunk(x_prompt, p, tables), _trunk(x_sample, p, tables))
```

```python
import functools
import math

import jax
import jax.numpy as jnp
from jax import lax
from jax.experimental import pallas as pl
from jax.experimental.pallas import tpu as pltpu

HEAD_DIM = 64
N_HEADS = 16
N_KV_HEADS = 4
GROUP = N_HEADS // N_KV_HEADS
GRID_W = 64
ROPE_AXIS_DIM = HEAD_DIM // 2
ROPE_HALF = ROPE_AXIS_DIM // 2
ROPE_THETA = 10000.0
CONV_WIDTH = 31
CONV_PAD = CONV_WIDTH // 2
RMS_EPS = 1e-6
LN_EPS = 1e-5

LANES = 128
SUBLANES = 8
MXU_N = 256
HALO = 16
BF16_SUBLANES = 16
V_ROWS = HEAD_DIM + BF16_SUBLANES
VMEM_LIMIT = 56 * 1024 * 1024

F32 = jnp.float32
BF16 = jnp.bfloat16


def _const_spec(shape):
    zeros = (0,) * len(shape)
    return pl.BlockSpec(shape, lambda *_: zeros, pipeline_mode=pl.Buffered(1))


def _rms(x, g):
    ms = jnp.mean(x * x, axis=-1, keepdims=True)
    return x * lax.rsqrt(ms + RMS_EPS) * g


def _qkv_kernel(x_ref, g_ref, w_ref, gqk_ref, seg_ref, cos_ref, sa_ref, sb_ref,
                qT_ref, k_ref, vT_ref, *, tq):
    tm = x_ref.shape[1]
    q_dim = N_HEADS * HEAD_DIM
    qk_dim = q_dim + N_KV_HEADS * HEAD_DIM
    h = _rms(x_ref[0], g_ref[...]).astype(BF16)

    def project(c):
        return jnp.dot(h, w_ref[:, c * MXU_N:(c + 1) * MXU_N], preferred_element_type=F32)

    seg = seg_ref[...]
    cos, sa, sb = cos_ref[...], sa_ref[...], sb_ref[...]
    z_next = project(0)
    for c in range(qk_dim // MXU_N):
        z, z_next = z_next, project(c + 1)
        y = z * z
        y_hi = y.astype(BF16)
        y_lo = (y - y_hi.astype(F32)).astype(BF16)
        ms = (jnp.dot(y_hi, seg, preferred_element_type=F32)
              + jnp.dot(y_lo, seg, preferred_element_type=F32))
        zn = z * lax.rsqrt(ms + RMS_EPS) * gqk_ref[:, c * MXU_N:(c + 1) * MXU_N]
        for half in range(MXU_N // LANES):
            zz = zn[:, half * LANES:(half + 1) * LANES]
            r = (zz * cos
                 + pltpu.roll(zz, ROPE_HALF, 1) * sa
                 + pltpu.roll(zz, LANES - ROPE_HALF, 1) * sb)
            col = c * MXU_N + half * LANES
            if col < q_dim:
                rT = r.T.astype(BF16)
                for hh in range(LANES // HEAD_DIM):
                    g = half * (LANES // HEAD_DIM) + hh
                    for tt in range(tm // tq):
                        dst = (tt * GROUP + g) * tq
                        qT_ref[0, c, :, dst:dst + tq] = rT[hh * HEAD_DIM:(hh + 1) * HEAD_DIM,
                                                           tt * tq:(tt + 1) * tq]
            else:
                kv = (col - q_dim) // HEAD_DIM
                k_ref[0, kv] = r[:, :HEAD_DIM].astype(BF16)
                k_ref[0, kv + 1] = r[:, HEAD_DIM:].astype(BF16)
    assert w_ref.shape[1] - qk_dim == MXU_N
    vT = z_next.T.astype(BF16)
    ones_rows = (lax.broadcasted_iota(jnp.int32, (V_ROWS - HEAD_DIM, tm), 0) == 0).astype(BF16)
    for kv in range(N_KV_HEADS):
        vT_ref[0, kv, :HEAD_DIM, :] = vT[kv * HEAD_DIM:(kv + 1) * HEAD_DIM]
        vT_ref[0, kv, HEAD_DIM:, :] = ones_rows


def _qkv_call(x, g, w_qkv, gqk, seg, cos, sa, sb, tm, tq):
    B, S, D = x.shape
    n_s = S // tm
    qkv_dim = w_qkv.shape[1]
    tab_spec = pl.BlockSpec((tm, LANES), lambda b, i: (i, 0))
    return pl.pallas_call(
        functools.partial(_qkv_kernel, tq=tq),
        grid=(B, n_s),
        in_specs=[
            pl.BlockSpec((1, tm, D), lambda b, i: (b, i, 0)),
            _const_spec((1, D)),
            _const_spec((D, qkv_dim)),
            _const_spec((1, gqk.shape[1])),
            _const_spec((MXU_N, MXU_N)),
            tab_spec, tab_spec, tab_spec,
        ],
        out_specs=[
            pl.BlockSpec((1, N_KV_HEADS, HEAD_DIM, GROUP * tm), lambda b, i: (b, 0, 0, i)),
            pl.BlockSpec((1, N_KV_HEADS, tm, HEAD_DIM), lambda b, i: (b, 0, i, 0)),
            pl.BlockSpec((1, N_KV_HEADS, V_ROWS, tm), lambda b, i: (b, 0, 0, i)),
        ],
        out_shape=[
            jax.ShapeDtypeStruct((B, N_KV_HEADS, HEAD_DIM, GROUP * S), BF16),
            jax.ShapeDtypeStruct((B, N_KV_HEADS, S, HEAD_DIM), BF16),
            jax.ShapeDtypeStruct((B, N_KV_HEADS, V_ROWS, S), BF16),
        ],
        compiler_params=pltpu.CompilerParams(
            dimension_semantics=("parallel", "parallel"),
            vmem_limit_bytes=VMEM_LIMIT),
        name="qkv_rope",
    )(x, g, w_qkv, gqk, seg, cos, sa, sb)


NEG_BIG = -1e30
SCORE_BOUND_MAX = 60.0
BF16_ROUNDING_MARGIN = 1.02


S_SLOTS = 4
P_SLOTS = 2
LOOKAHEAD = 2
UNROLL = 8
FLASH_TQ = 512
FLASH_TK = 256


def _flash_kernel(qT_ref, k_ref, vT_ref, o_ref, s_ref, p_ref, mx_ref, m_ref, acc_ref,
                  *, tk, tq):
    S = k_ref.shape[2]
    n = S // tk

    def scores(c, c_static):
        start = pl.multiple_of(c * tk, tk)
        s = jnp.dot(k_ref[0, 0, pl.ds(start, tk), :], qT_ref[0, 0],
                    preferred_element_type=F32)
        s_ref[c_static % S_SLOTS] = s
        mx_ref[c_static % S_SLOTS] = jnp.max(s, axis=0, keepdims=True)

    def pv(c, c_static):
        start = pl.multiple_of(c * tk, tk)
        return jnp.dot(vT_ref[0, 0, :, pl.ds(start, tk)], p_ref[c_static % P_SLOTS],
                       preferred_element_type=F32)

    def step(base, u, prefetch):
        i = base + u
        if prefetch:
            scores(i + LOOKAHEAD, u + LOOKAHEAD)
        pv_prev = pv(jnp.maximum(i - 1, 0), u - 1)
        m_old = m_ref[...]
        m_new = jnp.maximum(m_old, mx_ref[u % S_SLOTS])
        alpha = jnp.exp2(m_old - m_new)
        p_ref[u % P_SLOTS] = jnp.exp2(s_ref[u % S_SLOTS] - m_new).astype(BF16)
        m_ref[...] = m_new
        acc_ref[...] = (acc_ref[...] + pv_prev) * alpha

    m_ref[...] = jnp.full(m_ref.shape, NEG_BIG, F32)
    acc_ref[...] = jnp.zeros(acc_ref.shape, F32)
    p_ref[(-1) % P_SLOTS] = jnp.zeros(p_ref.shape[1:], BF16)
    for c in range(LOOKAHEAD):
        scores(c, c)

    def group(j, carry):
        for u in range(UNROLL):
            step(j * UNROLL, u, True)
        return carry

    lax.fori_loop(0, n // UNROLL - 1, group, 0)
    for u in range(UNROLL):
        step(n - UNROLL, u, u + LOOKAHEAD < UNROLL)
    acc = acc_ref[...] + pv(n - 1, n - 1)
    _flash_finish(acc[:HEAD_DIM], acc[HEAD_DIM:HEAD_DIM + 1], o_ref, tq)


def _flash_finish(o_t, denom, o_ref, tq):
    o = o_t * (1.0 / denom)
    o_ref[0] = jnp.concatenate(
        [o[:, g * tq:(g + 1) * tq].T for g in range(GROUP)], axis=1).astype(BF16)


def _flash_bounded_kernel(qT_ref, k_ref, vT_ref, o_ref, p_ref, l_ref, acc_ref, *, tk, tq):
    S = k_ref.shape[2]
    n = S // tk

    def numerators(c, c_static):
        start = pl.multiple_of(c * tk, tk)
        s = jnp.dot(k_ref[0, 0, pl.ds(start, tk), :], qT_ref[0, 0],
                    preferred_element_type=F32)
        p = jnp.exp2(s)
        l_ref[...] += sum(p[j * SUBLANES:(j + 1) * SUBLANES] for j in range(tk // SUBLANES))
        p_ref[c_static % S_SLOTS] = p.astype(BF16)

    def step(base, u, prefetch):
        i = base + u
        if prefetch:
            numerators(i + LOOKAHEAD, u + LOOKAHEAD)
        start = pl.multiple_of(i * tk, tk)
        acc_ref[...] += jnp.dot(vT_ref[0, 0, :HEAD_DIM, pl.ds(start, tk)], p_ref[u % S_SLOTS],
                                preferred_element_type=F32)

    acc_ref[...] = jnp.zeros(acc_ref.shape, F32)
    l_ref[...] = jnp.zeros(l_ref.shape, F32)
    for c in range(LOOKAHEAD):
        numerators(c, c)

    def group(j, carry):
        for u in range(UNROLL):
            step(j * UNROLL, u, True)
        return carry

    lax.fori_loop(0, n // UNROLL - 1, group, 0)
    for u in range(UNROLL):
        step(n - UNROLL, u, u + LOOKAHEAD < UNROLL)
    _flash_finish(acc_ref[...], jnp.sum(l_ref[...], axis=0, keepdims=True), o_ref, tq)


def _flash_call(qT, k, vT, tq, tk, online):
    B, _, S, _ = k.shape
    n_cols = GROUP * tq
    assert S % (UNROLL * tk) == 0 and UNROLL % S_SLOTS == 0 and UNROLL % P_SLOTS == 0
    assert LOOKAHEAD < S_SLOTS
    if online:
        body = _flash_kernel
        scratch = [
            pltpu.VMEM((S_SLOTS, tk, n_cols), F32),
            pltpu.VMEM((P_SLOTS, tk, n_cols), BF16),
            pltpu.VMEM((S_SLOTS, 1, n_cols), F32),
            pltpu.VMEM((1, n_cols), F32),
            pltpu.VMEM((V_ROWS, n_cols), F32),
        ]
    else:
        body = _flash_bounded_kernel
        scratch = [
            pltpu.VMEM((S_SLOTS, tk, n_cols), BF16),
            pltpu.VMEM((SUBLANES, n_cols), F32),
            pltpu.VMEM((HEAD_DIM, n_cols), F32),
        ]
    return pl.pallas_call(
        functools.partial(body, tk=tk, tq=tq),
        grid=(B, N_KV_HEADS, S // tq),
        in_specs=[
            pl.BlockSpec((1, 1, HEAD_DIM, n_cols), lambda b, j, i: (b, j, 0, i)),
            pl.BlockSpec((1, 1, S, HEAD_DIM), lambda b, j, i: (b, j, 0, 0)),
            pl.BlockSpec((1, 1, V_ROWS, S), lambda b, j, i: (b, j, 0, 0)),
        ],
        out_specs=pl.BlockSpec((1, tq, GROUP * HEAD_DIM), lambda b, j, i: (b, i, j)),
        out_shape=jax.ShapeDtypeStruct((B, S, N_HEADS * HEAD_DIM), BF16),
        scratch_shapes=scratch,
        compiler_params=pltpu.CompilerParams(
            dimension_semantics=("parallel", "parallel", "parallel"),
            vmem_limit_bytes=VMEM_LIMIT),
        name="flash_online" if online else "flash_bounded",
    )(qT, k, vT)


def _ffn(x, g_ref, wg_ref, wu_ref, wd_ref, act_ref):
    h = _rms(x, g_ref[...]).astype(BF16)
    f = wg_ref.shape[1]
    for c in range(f // MXU_N):
        cols = slice(c * MXU_N, (c + 1) * MXU_N)
        gate = jnp.dot(h, wg_ref[:, cols], preferred_element_type=F32)
        up = jnp.dot(h, wu_ref[:, cols], preferred_element_type=F32)
        act_ref[:, cols] = (gate * jax.nn.sigmoid(gate) * up).astype(BF16)
    return x + jnp.dot(act_ref[...], wd_ref[...], preferred_element_type=F32)


def _attn_out_ffn_kernel(x_ref, o_ref, wo_ref, g_ref, wg_ref, wu_ref, wd_ref,
                         y_ref, act_ref):
    x1 = x_ref[...] + jnp.dot(o_ref[...], wo_ref[...], preferred_element_type=F32)
    y_ref[...] = _ffn(x1, g_ref, wg_ref, wu_ref, wd_ref, act_ref)


def _attn_out_ffn_call(x, o, wo, g, wg, wu, wd, tm):
    T, D = x.shape
    f = wg.shape[1]
    row_spec = pl.BlockSpec((tm, D), lambda i: (i, 0))
    return pl.pallas_call(
        _attn_out_ffn_kernel,
        grid=(T // tm,),
        in_specs=[row_spec, row_spec, _const_spec((D, D)), _const_spec((1, D)),
                  _const_spec((D, f)), _const_spec((D, f)), _const_spec((f, D))],
        out_specs=row_spec,
        out_shape=jax.ShapeDtypeStruct((T, D), F32),
        scratch_shapes=[pltpu.VMEM((tm, f), BF16)],
        compiler_params=pltpu.CompilerParams(
            dimension_semantics=("parallel",),
            vmem_limit_bytes=VMEM_LIMIT),
        name="attn_out_ffn",
    )(x, o, wo, g, wg, wu, wd)


def _conv_ffn_kernel(xm_ref, xp_ref, xn_ref, cg_ref, win_ref, bin_ref, dww_ref, dwb_ref,
                     lng_ref, lnb_ref, wout_ref, bout_ref, fg_ref, wg_ref, wu_ref, wd_ref,
                     fin_ref, y_ref, u_ref, c_ref, x3_ref, act_ref, *, rc, tiles_per_seq):
    ts, D = xm_ref.shape[1], xm_ref.shape[2]
    i = pl.program_id(0)
    tile = jnp.minimum(i, pl.num_programs(0) - 2)
    pos = tile % tiles_per_seq

    @pl.when(i == 0)
    def _():
        x3_ref[1] = jnp.zeros((ts, D), F32)

    xm = xm_ref[0]
    xe = jnp.concatenate([xp_ref[0], xm, xn_ref[0]], axis=0)
    h = _rms(xe, cg_ref[...]).astype(BF16)
    u = jnp.dot(h, win_ref[...], preferred_element_type=F32) + bin_ref[...]
    u = u[:, :D] * jax.nn.sigmoid(u[:, D:])
    row = lax.broadcasted_iota(jnp.int32, (ts + 2 * HALO, 1), 0)
    valid = jnp.logical_and(jnp.logical_or(row >= HALO, pos > 0),
                            jnp.logical_or(row < ts + HALO, pos < tiles_per_seq - 1))
    u_ref[...] = jnp.where(valid, u, 0.0)

    x4 = _ffn(x3_ref[(i + 1) % 2], fg_ref, wg_ref, wu_ref, wd_ref, act_ref)
    y_ref[0] = _rms(x4, fin_ref[...])

    off = HALO - CONV_PAD
    win_rows = rc + 2 * HALO
    part_rows = rc + SUBLANES
    for r in range(ts // rc):
        for c in range(D // LANES):
            cols = slice(c * LANES, (c + 1) * LANES)
            window = u_ref[r * rc:r * rc + win_rows, cols]
            acc = jnp.broadcast_to(dwb_ref[:, cols], (rc, LANES))
            for phase in range(SUBLANES):
                part = None
                for base in range(0, 2 * HALO, SUBLANES):
                    k = base + phase - off
                    if 0 <= k < CONV_WIDTH:
                        term = window[base:base + part_rows] * dww_ref[k:k + 1, cols]
                        part = term if part is None else part + term
                if phase:
                    part = pltpu.roll(part, part_rows - phase, 0)
                acc = acc + part[:rc]
            c_ref[r * rc:(r + 1) * rc, cols] = acc
    v = c_ref[...]
    mu = jnp.mean(v, axis=-1, keepdims=True)
    vc = v - mu
    var = jnp.mean(vc * vc, axis=-1, keepdims=True)
    v = vc * lax.rsqrt(var + LN_EPS) * lng_ref[...] + lnb_ref[...]
    v = (v * jax.nn.sigmoid(v)).astype(BF16)
    x3_ref[i % 2] = xm + jnp.dot(v, wout_ref[...], preferred_element_type=F32) + bout_ref[...]


def _conv_ffn_call(x, cg, win, bin_, dww, dwb, lng, lnb, wout, bout, fg, wg, wu, wd, fin, ts, rc):
    B, S, D = x.shape
    f = wg.shape[1]
    hb = ts // HALO
    n_hb = S // HALO
    n_seq = S // ts
    n_tiles = B * n_seq

    def conv_tile(i):
        t = jnp.minimum(i, n_tiles - 1)
        return t // n_seq, t % n_seq

    def main_map(i):
        b, pos = conv_tile(i)
        return b, pos, 0

    def prev_halo_map(i):
        b, pos = conv_tile(i)
        return b, jnp.maximum(pos * hb - 1, 0), 0

    def next_halo_map(i):
        b, pos = conv_tile(i)
        return b, jnp.minimum((pos + 1) * hb, n_hb - 1), 0

    def out_map(i):
        t = jnp.maximum(i - 1, 0)
        return t // n_seq, t % n_seq, 0

    return pl.pallas_call(
        functools.partial(_conv_ffn_kernel, rc=rc, tiles_per_seq=n_seq),
        grid=(n_tiles + 1,),
        in_specs=[
            pl.BlockSpec((1, ts, D), main_map),
            pl.BlockSpec((1, HALO, D), prev_halo_map),
            pl.BlockSpec((1, HALO, D), next_halo_map),
            _const_spec((1, D)), _const_spec((D, 2 * D)), _const_spec((1, 2 * D)),
            _const_spec(dww.shape), _const_spec((1, D)), _const_spec((1, D)), _const_spec((1, D)),
            _const_spec((D, D)), _const_spec((1, D)), _const_spec((1, D)),
            _const_spec((D, f)), _const_spec((D, f)), _const_spec((f, D)), _const_spec((1, D)),
        ],
        out_specs=pl.BlockSpec((1, ts, D), out_map),
        out_shape=jax.ShapeDtypeStruct((B, S, D), F32),
        scratch_shapes=[
            pltpu.VMEM((ts + 2 * HALO, D), F32),
            pltpu.VMEM((ts, D), F32),
            pltpu.VMEM((2, ts, D), F32),
            pltpu.VMEM((ts, f), BF16),
        ],
        compiler_params=pltpu.CompilerParams(
            dimension_semantics=("arbitrary",),
            vmem_limit_bytes=VMEM_LIMIT),
        name="conv_ffn_final",
    )(x, x, x, cg, win, bin_, dww, dwb, lng, lnb, wout, bout, fg, wg, wu, wd, fin)


def _rope_tables(seq_len):
    t = jnp.arange(seq_len, dtype=jnp.int32)
    row_ids = (t // GRID_W).astype(F32)
    col_ids = (t % GRID_W).astype(F32)
    inv_freq = ROPE_THETA ** (-jnp.arange(0, ROPE_AXIS_DIM, 2, dtype=F32) / ROPE_AXIS_DIM)
    ang_r = row_ids[:, None] * inv_freq[None, :]
    ang_c = col_ids[:, None] * inv_freq[None, :]
    zero = jnp.zeros_like(ang_r)
    cos = jnp.concatenate([jnp.cos(ang_r)] * 2 + [jnp.cos(ang_c)] * 2, axis=-1)
    sa = jnp.concatenate([zero, jnp.sin(ang_r), zero, jnp.sin(ang_c)], axis=-1)
    sb = jnp.concatenate([-jnp.sin(ang_r), zero, -jnp.sin(ang_c), zero], axis=-1)
    reps = LANES // HEAD_DIM
    return tuple(jnp.tile(a, (1, reps)) for a in (cos, sa, sb))


def _trunk(x, p, tables):
    B, S, D = x.shape
    cos, sa, sb = (t[:S] for t in tables)
    qT, k, vT = _qkv_call(x, p["attn_g"], p["w_qkv"], p["gqk"], p["seg"], cos, sa, sb,
                          tm=512, tq=FLASH_TQ)
    o = lax.cond(p["score_bound"] <= SCORE_BOUND_MAX,
                 functools.partial(_flash_call, tq=FLASH_TQ, tk=FLASH_TK, online=False),
                 functools.partial(_flash_call, tq=FLASH_TQ, tk=FLASH_TK, online=True),
                 qT, k, vT)
    x2 = _attn_out_ffn_call(x.reshape(B * S, D), o.reshape(B * S, D), p["w_o"], p["ffn_g0"],
                            p["wg0"], p["wu0"], p["wd0"], tm=512)
    return _conv_ffn_call(x2.reshape(B, S, D), p["conv_g"], p["w_in"], p["b_in"], p["dw_w"],
                          p["dw_b"], p["ln_g"], p["ln_b"], p["w_out"], p["b_out"], p["ffn_g1"],
                          p["wg1"], p["wu1"], p["wd1"], p["fin_g"], ts=512, rc=64)


def kernel(x_prompt, x_sample, attn_norm_g, w_qkv, q_norm_g, k_norm_g, w_o, conv_norm_g, conv_w_in, conv_b_in, dw_w, dw_b, conv_ln_g, conv_ln_b, conv_w_out, conv_b_out, ffn_norm_g, w_gate, w_up, w_down, final_norm_g):
    row = lambda v: v.reshape(1, -1).astype(F32)
    scale = math.log2(math.e) / math.sqrt(HEAD_DIM)
    score_bound = (BF16_ROUNDING_MARGIN * HEAD_DIM * scale
                   * jnp.max(jnp.abs(q_norm_g[0])) * jnp.max(jnp.abs(k_norm_g[0])))
    gqk = jnp.concatenate([jnp.tile(q_norm_g[0] * scale, N_HEADS),
                           jnp.tile(k_norm_g[0], N_KV_HEADS)]).reshape(1, -1)
    lane = jnp.arange(MXU_N) // HEAD_DIM
    seg = jnp.where(lane[:, None] == lane[None, :], 1.0 / HEAD_DIM, 0.0).astype(BF16)
    p = dict(
        attn_g=row(attn_norm_g[0]), w_qkv=w_qkv[0].astype(BF16), gqk=gqk, seg=seg,
        w_o=w_o[0].astype(BF16), ffn_g0=row(ffn_norm_g[0]),
        wg0=w_gate[0].astype(BF16), wu0=w_up[0].astype(BF16), wd0=w_down[0].astype(BF16),
        conv_g=row(conv_norm_g[0]), w_in=conv_w_in[0].astype(BF16), b_in=row(conv_b_in[0]),
        dw_w=jnp.pad(dw_w[0], ((0, 1), (0, 0))), dw_b=row(dw_b[0]),
        ln_g=row(conv_ln_g[0]), ln_b=row(conv_ln_b[0]),
        w_out=conv_w_out[0].astype(BF16), b_out=row(conv_b_out[0]), ffn_g1=row(ffn_norm_g[1]),
        wg1=w_gate[1].astype(BF16), wu1=w_up[1].astype(BF16), wd1=w_down[1].astype(BF16),
        fin_g=row(final_norm_g), score_bound=score_bound,
    )
    tables = _rope_tables(max(x_prompt.shape[1], x_sample.shape[1]))
    return (_trunk(x_prompt, p, tables), _trunk(x_sample, p, tables))
```
